```python
import math
import jax, jax.numpy as jnp
from jax import lax
import numpy as np

D_MODEL = 1024
BATCH = 2
SEQ = 8192
DEPTH = 2

GRID_W = 64
CTX_LEN = 256
EPS = 1e-6
N_MOD = 6
CONV_W = 4
RG_HEADS = 6
RG_HEAD_DIM = 128
RG_WIDTH = RG_HEADS * RG_HEAD_DIM
RG_C = 8.0
FN_GROUPS = 4
FN_GROUP_DIM = 64
FN_WIDTH = FN_GROUPS * FN_GROUP_DIM
EVEN_IN = 2 * RG_WIDTH + FN_WIDTH
EVEN_MIX = RG_WIDTH + FN_WIDTH
ML_HEADS = 4
ML_HEAD_DIM = 128
ML_WIDTH = ML_HEADS * ML_HEAD_DIM
ML_CHUNK = 128
ML_N_GATES = 2 * 2 * ML_HEADS
ML_COLS = 4 * ML_WIDTH + ML_N_GATES
MLA_HEADS = 8
MLA_NOPE = 64
MLA_ROPE = 32
MLA_V = 64
MLA_Q_LORA = 384
MLA_KV_LORA = 256
MLA_WIDTH = MLA_HEADS * MLA_V
MLA_COLS = MLA_Q_LORA + MLA_KV_LORA + MLA_ROPE
ODD_IN = ML_COLS + MLA_COLS
ODD_MIX = ML_WIDTH + MLA_WIDTH
ATTN_BLOCK = 128
ROPE_THETA = 10000.0
PEER_HEADS = 8
PEER_KEY_DIM = 256
PEER_N_KEYS = 128
PEER_N_EXPERTS = PEER_N_KEYS * PEER_N_KEYS
PEER_TOPK = 16
PEER_BLOCK = 128
N_EVEN = (DEPTH + 1) // 2
N_ODD = DEPTH // 2

kernel_name = "hybrid_rglru_fnet_mlstm_mla_peer_dit"

f32 = jnp.float32


def rmsnorm(x, g):
    xf = x.astype(f32)
    y = xf * lax.rsqrt(jnp.mean(xf * xf, axis=-1, keepdims=True) + EPS)
    return (y * g.astype(f32)).astype(x.dtype)


def modulate(h, shift, scale):
    return h * (1.0 + scale) + shift


def centred_dwconv(x, w, b):
    T = x.shape[1]
    lo = CONV_W // 2
    xp = jnp.pad(x, ((0, 0), (lo, CONV_W - 1 - lo), (0, 0)))
    y = xp[:, 0:T] * w[0]
    for j in range(1, CONV_W):
        y = y + xp[:, j:j + T] * w[j]
    return y + b


def linear_scan(a, u, h0):
    def comb(l, r):
        return (l[0] * r[0], r[0] * l[1] + r[1])
    A, Bc = lax.associative_scan(comb, (a, u), axis=1)
    h = A * h0[:, None, :] + Bc
    return h, h[:, -1]


def rglru_direction(xc, wa, ba, wx, bx, lam, h0):
    B_, T, _ = xc.shape
    xh = xc.reshape(B_, T, RG_HEADS, RG_HEAD_DIM)
    r = jax.nn.sigmoid(jnp.einsum('bthd,hde->bthe', xh, wa).reshape(B_, T, RG_WIDTH) + ba)
    i = jax.nn.sigmoid(jnp.einsum('bthd,hde->bthe', xh, wx).reshape(B_, T, RG_WIDTH) + bx)
    log_a = -RG_C * r * jax.nn.softplus(-lam)
    a = jnp.exp(log_a)
    u = jnp.sqrt(-jnp.expm1(2.0 * log_a)) * (i * xc)
    return linear_scan(a, u, h0)


def rglru_mixer(xr_ctx, xr_lat, conv_w, conv_b, wa, ba, wx, bx, lam):
    xc_ctx = centred_dwconv(xr_ctx, conv_w, conv_b).astype(f32)
    xc_lat = centred_dwconv(xr_lat, conv_w, conv_b).astype(f32)
    B_ = xr_lat.shape[0]
    out_ctx = jnp.zeros_like(xc_ctx)
    out_lat = jnp.zeros_like(xc_lat)
    for d in range(2):
        fl = (lambda t: jnp.flip(t, axis=1)) if d else (lambda t: t)
        h0 = jnp.zeros((B_, RG_WIDTH), f32)
        h_c, h_end = rglru_direction(fl(xc_ctx), wa[d], ba[d], wx[d], bx[d], lam[d], h0)
        h_l, _ = rglru_direction(fl(xc_lat), wa[d], ba[d], wx[d], bx[d], lam[d], h_end)
        out_ctx = out_ctx + fl(h_c)
        out_lat = out_lat + fl(h_l)
    return out_ctx, out_lat


def fourier_mixer(xf, w, b):
    B_, T, _ = xf.shape
    xg = xf.astype(f32).reshape(B_, T, FN_GROUPS, FN_GROUP_DIM)
    z = jnp.fft.fft2(xg, axes=(1, 3), norm='ortho').real
    return jnp.einsum('btgd,gde->btge', z, w).reshape(B_, T, FN_WIDTH) + b


def even_mixer(u_ctx, u_lat, w_in, w_out, conv_w, conv_b, wa, ba, wx, bx, lam, fw, fb, need_ctx_out):
    R = RG_WIDTH
    p_ctx = u_ctx @ w_in
    p_lat = u_lat @ w_in
    rnn_ctx, rnn_lat = rglru_mixer(p_ctx[..., :R], p_lat[..., :R], conv_w, conv_b, wa, ba, wx, bx, lam)

    def merge(p, rnn):
        y_a = rnn.astype(p.dtype) * jax.nn.gelu(p[..., R:2 * R])
        y_b = fourier_mixer(p[..., 2 * R:], fw, fb).astype(p.dtype)
        return jnp.concatenate([y_a, y_b], axis=-1) @ w_out

    y_lat = merge(p_lat, rnn_lat)
    y_ctx = merge(p_ctx, rnn_ctx) if need_ctx_out else None
    return y_ctx, y_lat


def mlstm_chunkwise(q, k, v, ig, lf, state):
    B_, H, T, Dh = q.shape
    nc = T // ML_CHUNK
    L = ML_CHUNK

    def chunks(t):
        return jnp.moveaxis(t.reshape((B_, H, nc, L) + t.shape[3:]), 2, 0)

    causal = jnp.tril(jnp.ones((L, L), dtype=bool))

    def step(carry, inp):
        C, n, m = carry
        q_, k_, v_, i_, f_ = inp
        b = jnp.cumsum(f_, axis=-1)
        Dm = jnp.where(causal, b[..., :, None] - b[..., None, :] + i_[..., None, :], -jnp.inf)
        inter = b + m[..., None]
        m_row = jnp.maximum(jnp.max(Dm, axis=-1), inter)
        s = jnp.einsum('bhld,bhsd->bhls', q_, k_) * jnp.exp(Dm - m_row[..., None])
        w_inter = jnp.exp(inter - m_row)
        num = jnp.einsum('bhls,bhsd->bhld', s, v_) + w_inter[..., None] * jnp.einsum('bhde,bhle->bhld', C, q_)
        den = jnp.sum(s, axis=-1) + w_inter * jnp.einsum('bhe,bhle->bhl', n, q_)
        h = num / jnp.maximum(jnp.abs(den), jnp.exp(-m_row))[..., None]
        bL = b[..., -1]
        log_wk = bL[..., None] - b + i_
        m_new = jnp.maximum(bL + m, jnp.max(log_wk, axis=-1))
        wk = jnp.exp(log_wk - m_new[..., None])
        decay = jnp.exp(bL + m - m_new)
        C_new = decay[..., None, None] * C + jnp.einsum('bhs,bhsd,bhse->bhde', wk, v_, k_)
        n_new = decay[..., None] * n + jnp.einsum('bhs,bhse->bhe', wk, k_)
        return (C_new, n_new, m_new), h

    state, hs = lax.scan(step, state, (chunks(q), chunks(k), chunks(v), chunks(ig), chunks(lf)))
    return jnp.moveaxis(hs, 0, 2).reshape(B_, H, T, Dh), state


def mlstm_prep(p, conv_w, conv_b, gate_b):
    B_, T, _ = p.shape
    W = ML_WIDTH
    qk = jax.nn.silu(centred_dwconv(p[..., :2 * W], conv_w, conv_b))

    def heads(t):
        return t.reshape(B_, T, ML_HEADS, ML_HEAD_DIM).transpose(0, 2, 1, 3).astype(f32)

    q = heads(qk[..., :W]) * (ML_HEAD_DIM ** -0.5)
    k = heads(qk[..., W:])
    v = heads(p[..., 2 * W:3 * W])
    o = p[..., 3 * W:4 * W]
    g = (p[..., 4 * W:4 * W + ML_N_GATES] + gate_b).astype(f32).reshape(B_, T, 2, 2, ML_HEADS)
    g = g.transpose(2, 3, 0, 4, 1)
    return q, k, v, o, g[:, 0], jax.nn.log_sigmoid(g[:, 1])


def mlstm_mixer(p_ctx, p_lat, conv_w, conv_b, gate_b, head_g, need_ctx_out):
    qc, kc, vc, oc, igc, lfc = mlstm_prep(p_ctx, conv_w, conv_b, gate_b)
    ql, kl, vl, ol, igl, lfl = mlstm_prep(p_lat, conv_w, conv_b, gate_b)
    B_ = p_lat.shape[0]
    h_ctx = jnp.zeros_like(vc)
    h_lat = jnp.zeros_like(vl)
    for d in range(2):
        fl = (lambda t: jnp.flip(t, axis=2)) if d else (lambda t: t)
        st0 = (jnp.zeros((B_, ML_HEADS, ML_HEAD_DIM, ML_HEAD_DIM), f32),
               jnp.zeros((B_, ML_HEADS, ML_HEAD_DIM), f32),
               jnp.zeros((B_, ML_HEADS), f32))
        hc, st = mlstm_chunkwise(fl(qc), fl(kc), fl(vc), fl(igc[d]), fl(lfc[d]), st0)
        hl, _ = mlstm_chunkwise(fl(ql), fl(kl), fl(vl), fl(igl[d]), fl(lfl[d]), st)
        h_ctx = h_ctx + fl(hc)
        h_lat = h_lat + fl(hl)

    def readout(h, o):
        T = h.shape[2]
        hn = rmsnorm(h, head_g.reshape(1, ML_HEADS, 1, ML_HEAD_DIM))
        hn = hn.transpose(0, 2, 1, 3).reshape(B_, T, ML_WIDTH)
        return jax.nn.sigmoid(o) * hn.astype(o.dtype)

    y_lat = readout(h_lat, ol)
    y_ctx = readout(h_ctx, oc) if need_ctx_out else None
    return y_ctx, y_lat


def axial_rope_angles(row, col):
    half = MLA_ROPE // 2
    inv = ROPE_THETA ** (-jnp.arange(0, half, 2, dtype=f32) / half)
    ang = jnp.concatenate([row[:, None] * inv, col[:, None] * inv], axis=-1)
    return jnp.cos(ang), jnp.sin(ang)


def apply_rot(x, cos, sin):
    x1, x2 = x[..., 0::2], x[..., 1::2]
    return jnp.stack([x1 * cos - x2 * sin, x1 * sin + x2 * cos], axis=-1).reshape(x.shape).astype(x.dtype)


def mla_prep(p, q_norm_g, w_uq, kv_norm_g, w_ukv):
    B_, T, _ = p.shape
    cq = rmsnorm(p[..., :MLA_Q_LORA], q_norm_g)
    q = (cq @ w_uq).reshape(B_, T, MLA_HEADS, MLA_NOPE + MLA_ROPE)
    ckv = rmsnorm(p[..., MLA_Q_LORA:MLA_Q_LORA + MLA_KV_LORA], kv_norm_g)
    kv = (ckv @ w_ukv).reshape(B_, T, MLA_HEADS, MLA_NOPE + MLA_V)
    k_rope = p[..., MLA_Q_LORA + MLA_KV_LORA:]
    return q[..., :MLA_NOPE], q[..., MLA_NOPE:], kv[..., :MLA_NOPE], k_rope, kv[..., MLA_NOPE:]


def mla_attend(q_nope, q_rope, k_nope, k_rope, v):
    B_, Tq, H, _ = q_nope.shape
    nb = Tq // ATTN_BLOCK
    scale = (MLA_NOPE + MLA_ROPE) ** -0.5

    def blocks(t):
        return t.reshape(B_, nb, ATTN_BLOCK, H, t.shape[-1]).swapaxes(0, 1)

    def one_block(qs):
        qn, qr = qs
        s = jnp.einsum('bqhd,bkhd->bhqk', qn, k_nope) + jnp.einsum('bqhd,bkd->bhqk', qr, k_rope)
        pr = jax.nn.softmax(s.astype(f32) * scale, axis=-1).astype(v.dtype)
        return jnp.einsum('bhqk,bkhd->bqhd', pr, v)

    out = lax.map(one_block, (blocks(q_nope), blocks(q_rope)))
    return out.swapaxes(0, 1).reshape(B_, Tq, H * MLA_V)


def mla_mixer(p_ctx, p_lat, q_norm_g, w_uq, kv_norm_g, w_ukv, cos, sin, need_ctx_out):
    qn_c, qr_c, kn_c, kr_c, v_c = mla_prep(p_ctx, q_norm_g, w_uq, kv_norm_g, w_ukv)
    qn_l, qr_l, kn_l, kr_l, v_l = mla_prep(p_lat, q_norm_g, w_uq, kv_norm_g, w_ukv)
    qr_l = apply_rot(qr_l, cos[:, None, :], sin[:, None, :])
    kr_l = apply_rot(kr_l, cos, sin)
    kn = jnp.concatenate([kn_c, kn_l], axis=1)
    kr = jnp.concatenate([kr_c, kr_l], axis=1)
    v = jnp.concatenate([v_c, v_l], axis=1)
    y_lat = mla_attend(qn_l, qr_l, kn, kr, v)
    y_ctx = mla_attend(qn_c, qr_c, kn_c, kr_c, v_c) if need_ctx_out else None
    return y_ctx, y_lat


def odd_mixer(u_ctx, u_lat, w_in, w_out, conv_w, conv_b, gate_b, head_g,
              q_norm_g, w_uq, kv_norm_g, w_ukv, cos, sin, need_ctx_out):
    p_ctx = u_ctx @ w_in
    p_lat = u_lat @ w_in
    ml_ctx, ml_lat = mlstm_mixer(p_ctx[..., :ML_COLS], p_lat[..., :ML_COLS], conv_w, conv_b, gate_b, head_g, need_ctx_out)
    at_ctx, at_lat = mla_mixer(p_ctx[..., ML_COLS:], p_lat[..., ML_COLS:], q_norm_g, w_uq, kv_norm_g, w_ukv, cos, sin, need_ctx_out)
    y_lat = jnp.concatenate([ml_lat, at_lat], axis=-1) @ w_out
    y_ctx = (jnp.concatenate([ml_ctx, at_ctx], axis=-1) @ w_out) if need_ctx_out else None
    return y_ctx, y_lat


def peer_ffn(h, w_q, keys, u, v):
    B_, T, D = h.shape
    tok = h.reshape(-1, PEER_BLOCK, D)

    def one_block(xb):
        q = (xb @ w_q).reshape(PEER_BLOCK, PEER_HEADS, 2, PEER_KEY_DIM // 2)
        s = jnp.einsum('thpd,phkd->thpk', q, keys).astype(f32)
        vals, idx = lax.top_k(s, PEER_TOPK)
        cand = (vals[:, :, 0, :, None] + vals[:, :, 1, None, :]).reshape(PEER_BLOCK, PEER_HEADS, PEER_TOPK * PEER_TOPK)
        cidx = (idx[:, :, 0, :, None] * PEER_N_KEYS + idx[:, :, 1, None, :]).reshape(PEER_BLOCK, PEER_HEADS, PEER_TOPK * PEER_TOPK)
        top_s, pos = lax.top_k(cand, PEER_TOPK)
        e = jnp.take_along_axis(cidx, pos, axis=-1)
        g = jax.nn.softmax(top_s, axis=-1)
        ue = jnp.take(u, e, axis=0)
        ve = jnp.take(v, e, axis=0)
        act = jax.nn.gelu(jnp.einsum('thkd,td->thk', ue, xb))
        return jnp.einsum('thk,thkd->td', (g * act).astype(ve.dtype), ve)

    return lax.map(one_block, tok).reshape(B_, T, D)


def setup_inputs(seed: int = 0) -> dict:
    key = jax.random.key(seed)
    ks = iter(jax.random.split(key, 48))

    def nrm(shape, scale):
        return jax.random.normal(next(ks), shape, f32) * scale

    def gain(shape):
        return 1.0 + nrm(shape, 0.05)

    D = D_MODEL
    x = nrm((BATCH, SEQ, D), 1.0)
    c = nrm((BATCH, D), 1.0)
    ctx = nrm((BATCH, CTX_LEN, D), 1.0)
    c_ctx = nrm((D,), 1.0)
    mod_w = nrm((DEPTH, D, N_MOD * D), 0.5 * D ** -0.5)
    mod_b = nrm((DEPTH, N_MOD * D), 0.02)
    norm_mix_g = gain((DEPTH, D))
    norm_ffn_g = gain((DEPTH, D))
    ev_w_in = nrm((N_EVEN, D, EVEN_IN), D ** -0.5)
    ev_w_out = nrm((N_EVEN, EVEN_MIX, D), EVEN_MIX ** -0.5)
    rg_conv_w = nrm((N_EVEN, CONV_W, RG_WIDTH), 0.5)
    rg_conv_b = nrm((N_EVEN, RG_WIDTH), 0.02)
    rg_wa = nrm((N_EVEN, 2, RG_HEADS, RG_HEAD_DIM, RG_HEAD_DIM), RG_HEAD_DIM ** -0.5)
    rg_ba = nrm((N_EVEN, 2, RG_WIDTH), 0.02)
    rg_wx = nrm((N_EVEN, 2, RG_HEADS, RG_HEAD_DIM, RG_HEAD_DIM), RG_HEAD_DIM ** -0.5)
    rg_bx = nrm((N_EVEN, 2, RG_WIDTH), 0.02)
    a8 = jax.random.uniform(next(ks), (N_EVEN, 2, RG_WIDTH), f32, 0.9, 0.999)
    a = a8 ** (1.0 / RG_C)
    rg_lam = jnp.log(a) - jnp.log1p(-a)
    fn_w = nrm((N_EVEN, FN_GROUPS, FN_GROUP_DIM, FN_GROUP_DIM), FN_GROUP_DIM ** -0.5)
    fn_b = nrm((N_EVEN, FN_WIDTH), 0.02)
    od_w_in = nrm((N_ODD, D, ODD_IN), D ** -0.5)
    od_w_out = nrm((N_ODD, ODD_MIX, D), ODD_MIX ** -0.5)
    ml_conv_w = nrm((N_ODD, CONV_W, 2 * ML_WIDTH), 0.5)
    ml_conv_b = nrm((N_ODD, 2 * ML_WIDTH), 0.02)
    f_bias = 3.0 + 3.0 * jax.random.uniform(next(ks), (N_ODD, 2, 1, ML_HEADS), f32)
    i_bias = nrm((N_ODD, 2, 1, ML_HEADS), 0.1)
    ml_gate_b = jnp.concatenate([i_bias, f_bias], axis=2).reshape(N_ODD, ML_N_GATES)
    ml_head_g = gain((N_ODD, ML_WIDTH))
    mla_q_norm_g = gain((N_ODD, MLA_Q_LORA))
    mla_w_uq = nrm((N_ODD, MLA_Q_LORA, MLA_HEADS * (MLA_NOPE + MLA_ROPE)), MLA_Q_LORA ** -0.5)
    mla_kv_norm_g = gain((N_ODD, MLA_KV_LORA))
    mla_w_ukv = nrm((N_ODD, MLA_KV_LORA, MLA_HEADS * (MLA_NOPE + MLA_V)), MLA_KV_LORA ** -0.5)
    peer_w_q = nrm((DEPTH, D, PEER_HEADS * PEER_KEY_DIM), D ** -0.5)
    peer_keys = nrm((DEPTH, 2, PEER_HEADS, PEER_N_KEYS, PEER_KEY_DIM // 2), (PEER_KEY_DIM // 2) ** -0.5)
    peer_u = nrm((DEPTH, PEER_N_EXPERTS, D), D ** -0.5)
    peer_v = nrm((DEPTH, PEER_N_EXPERTS, D), 0.5)
    final_g = gain((D,))
    return {"x": x, "c": c, "ctx": ctx, "c_ctx": c_ctx, "mod_w": mod_w, "mod_b": mod_b,
            "norm_mix_g": norm_mix_g, "norm_ffn_g": norm_ffn_g, "ev_w_in": ev_w_in, "ev_w_out": ev_w_out,
            "rg_conv_w": rg_conv_w, "rg_conv_b": rg_conv_b, "rg_wa": rg_wa, "rg_ba": rg_ba,
            "rg_wx": rg_wx, "rg_bx": rg_bx, "rg_lam": rg_lam, "fn_w": fn_w, "fn_b": fn_b,
            "od_w_in": od_w_in, "od_w_out": od_w_out, "ml_conv_w": ml_conv_w, "ml_conv_b": ml_conv_b,
            "ml_gate_b": ml_gate_b, "ml_head_g": ml_head_g, "mla_q_norm_g": mla_q_norm_g,
            "mla_w_uq": mla_w_uq, "mla_kv_norm_g": mla_kv_norm_g, "mla_w_ukv": mla_w_ukv,
            "peer_w_q": peer_w_q, "peer_keys": peer_keys, "peer_u": peer_u, "peer_v": peer_v,
            "final_g": final_g}


def reference(x, c, ctx, c_ctx, mod_w, mod_b, norm_mix_g, norm_ffn_g, ev_w_in, ev_w_out,
              rg_conv_w, rg_conv_b, rg_wa, rg_ba, rg_wx, rg_bx, rg_lam, fn_w, fn_b,
              od_w_in, od_w_out, ml_conv_w, ml_conv_b, ml_gate_b, ml_head_g, mla_q_norm_g,
              mla_w_uq, mla_kv_norm_g, mla_w_ukv, peer_w_q, peer_keys, peer_u, peer_v, final_g):
    T = x.shape[1]
    ROWS = T // GRID_W
    row = jnp.repeat(jnp.arange(ROWS, dtype=f32), GRID_W)
    col = jnp.tile(jnp.arange(GRID_W, dtype=f32), ROWS)
    cos, sin = axial_rope_angles(row, col)
    sc = jax.nn.silu(c)
    sctx = jax.nn.silu(c_ctx)
    h_lat, h_ctx = x, ctx
    for l in range(DEPTH):
        need_ctx = l < DEPTH - 1
        m_lat = jnp.split((sc @ mod_w[l] + mod_b[l])[:, None, :], N_MOD, axis=-1)
        m_ctx = jnp.split((sctx @ mod_w[l] + mod_b[l])[None, None, :], N_MOD, axis=-1)
        u_lat = modulate(rmsnorm(h_lat, norm_mix_g[l]), m_lat[0], m_lat[1])
        u_ctx = modulate(rmsnorm(h_ctx, norm_mix_g[l]), m_ctx[0], m_ctx[1])
        j = l // 2
        if l % 2 == 0:
            y_ctx, y_lat = even_mixer(u_ctx, u_lat, ev_w_in[j], ev_w_out[j], rg_conv_w[j], rg_conv_b[j],
                                      rg_wa[j], rg_ba[j], rg_wx[j], rg_bx[j], rg_lam[j], fn_w[j], fn_b[j], need_ctx)
        else:
            y_ctx, y_lat = odd_mixer(u_ctx, u_lat, od_w_in[j], od_w_out[j], ml_conv_w[j], ml_conv_b[j],
                                     ml_gate_b[j], ml_head_g[j], mla_q_norm_g[j], mla_w_uq[j],
                                     mla_kv_norm_g[j], mla_w_ukv[j], cos, sin, need_ctx)
        h_lat = h_lat + m_lat[2] * y_lat
        f_lat = modulate(rmsnorm(h_lat, norm_ffn_g[l]), m_lat[3], m_lat[4])
        h_lat = h_lat + m_lat[5] * peer_ffn(f_lat, peer_w_q[l], peer_keys[l], peer_u[l], peer_v[l])
        if need_ctx:
            h_ctx = h_ctx + m_ctx[2] * y_ctx
            f_ctx = modulate(rmsnorm(h_ctx, norm_ffn_g[l]), m_ctx[3], m_ctx[4])
            h_ctx = h_ctx + m_ctx[5] * peer_ffn(f_ctx, peer_w_q[l], peer_keys[l], peer_u[l], peer_v[l])
    return rmsnorm(h_lat, final_g)
```

```python
import functools
import math

import numpy as np
import jax
import jax.numpy as jnp
from jax import lax
from jax.experimental import pallas as pl
from jax.experimental.pallas import tpu as pltpu

f32 = jnp.float32
bf16 = jnp.bfloat16
i32 = jnp.int32
HIGHEST = lax.Precision.HIGHEST

SUBLANES = 8
LANES = 128
VMEM_LIMIT = 48 * 1024 * 1024
VMEM_LIMIT_TABLE = 56 * 1024 * 1024

EPS = 1e-6
GRID_W = 64
GRID_SHIFT = 6
ROW_BLOCK = 256
CONV_W = 4
RG_HEADS, RG_HEAD_DIM = 6, 128
RG_WIDTH = RG_HEADS * RG_HEAD_DIM
RG_C = 8.0
FN_GROUPS, FN_GROUP_DIM = 4, 64
FN_WIDTH = FN_GROUPS * FN_GROUP_DIM
ML_HEADS, ML_HEAD_DIM = 4, 128
ML_WIDTH = ML_HEADS * ML_HEAD_DIM
ML_CHUNK = 128
ML_N_GATES = 2 * 2 * ML_HEADS
MLA_HEADS, MLA_NOPE, MLA_ROPE, MLA_V = 8, 64, 32, 64
MLA_Q_LORA, MLA_KV_LORA = 384, 256
MLA_HEAD_PAD = 128
ROPE_THETA = 10000.0
PEER_HEADS, PEER_KEY_DIM, PEER_N_KEYS, PEER_TOPK = 8, 256, 128, 16
PEER_PICKS = PEER_HEADS * PEER_TOPK
PEER_TB = 128
HI_MASK = -65536

OD_V, OD_O, OD_G, OD_CQ, OD_CKV, OD_KR, OD_N = 1024, 1536, 2048, 2176, 2560, 2816, 2944


def _cparams(sem, vmem=VMEM_LIMIT):
    return pltpu.CompilerParams(dimension_semantics=sem, vmem_limit_bytes=vmem)


def _gelu_tanh(x):
    return 0.5 * x * (1.0 + jnp.tanh(0.7978845608028654 * (x + 0.044715 * x * x * x)))


def _sigmoid(x):
    return 1.0 / (1.0 + jnp.exp(-x))


def _softplus(x):
    return jnp.maximum(x, 0.0) + jnp.log1p(jnp.exp(-jnp.abs(x)))


def _rms(x, g):
    return x * lax.rsqrt(jnp.mean(x * x, axis=-1, keepdims=True) + EPS) * g


def _mod_row(b, i, nctx_blk):
    return jnp.where(i < nctx_blk, 2, b)


def _mod_kernel(c_ref, w_ref, b_ref, o_ref):
    c = c_ref[...]
    s = c * _sigmoid(c)
    o_ref[0] = jnp.dot(s, w_ref[0], precision=HIGHEST, preferred_element_type=f32) + b_ref[0]


def modulation(cvec, mod_w, mod_b):
    depth, d, n = mod_w.shape
    tn = 1536
    return pl.pallas_call(
        _mod_kernel,
        grid=(depth, n // tn),
        in_specs=[
            pl.BlockSpec((SUBLANES, d), lambda l, j: (0, 0)),
            pl.BlockSpec((1, d, tn), lambda l, j: (l, 0, j)),
            pl.BlockSpec((1, 1, tn), lambda l, j: (l, 0, j)),
        ],
        out_specs=pl.BlockSpec((1, SUBLANES, tn), lambda l, j: (l, 0, j)),
        out_shape=jax.ShapeDtypeStruct((depth, SUBLANES, n), f32),
        compiler_params=_cparams(("arbitrary", "arbitrary")),
        name="modulation",
    )(cvec, mod_w, mod_b.reshape(depth, 1, n))


def _inproj_kernel(has_res, *refs):
    if has_res:
        h_ref, peer_ref, modp_ref, mod_ref, g_ref, w_ref, hout_ref, p_ref = refs
        h = h_ref[0] + modp_ref[0, 5:6, :] * peer_ref[0]
        hout_ref[0] = h
    else:
        h_ref, mod_ref, g_ref, w_ref, p_ref = refs
        h = h_ref[0]
    u = _rms(h, g_ref[...]) * (1.0 + mod_ref[0, 1:2, :]) + mod_ref[0, 0:1, :]
    p_ref[0] = jnp.dot(u.astype(bf16), w_ref[...], preferred_element_type=f32)


def inproj(h, mod, g, w, nctx_blk, peer=None, mod_prev=None):
    b, tt, d = h.shape
    n = w.shape[1]
    tr = ROW_BLOCK
    has_res = peer is not None
    row = pl.BlockSpec((1, tr, d), lambda bi, i: (bi, i, 0))
    modspec = pl.BlockSpec((1, SUBLANES, d), lambda bi, i: (_mod_row(bi, i, nctx_blk), 0, 0))
    in_specs = [row] + ([row, modspec] if has_res else []) + [
        modspec,
        pl.BlockSpec((1, d), lambda bi, i: (0, 0)),
        pl.BlockSpec((d, n), lambda bi, i: (0, 0)),
    ]
    pspec = pl.BlockSpec((1, tr, n), lambda bi, i: (bi, i, 0))
    pshape = jax.ShapeDtypeStruct((b, tt, n), f32)
    args = [h] + ([peer, mod_prev] if has_res else []) + [mod, g.reshape(1, d), w]
    return pl.pallas_call(
        functools.partial(_inproj_kernel, has_res),
        grid=(b, tt // tr),
        in_specs=in_specs,
        out_specs=[row, pspec] if has_res else pspec,
        out_shape=[jax.ShapeDtypeStruct((b, tt, d), f32), pshape] if has_res else pshape,
        compiler_params=_cparams(("arbitrary", "arbitrary")),
        name="inproj",
    )(*args)


def _conv_kernel(apply_silu, nctx_blk, nblk, prev_ref, cur_ref, next_ref, w_ref, b_ref, o_ref, ext):
    i = pl.program_id(1)
    tr = cur_ref.shape[1]
    has_prev = jnp.logical_and(i != 0, i != nctx_blk)
    has_next = jnp.logical_and(i != nctx_blk - 1, i != nblk - 1)
    ext[0:SUBLANES] = jnp.where(has_prev, prev_ref[0], 0.0)
    ext[SUBLANES:SUBLANES + tr] = cur_ref[0]
    ext[SUBLANES + tr:2 * SUBLANES + tr] = jnp.where(has_next, next_ref[0], 0.0)
    lo = CONV_W // 2
    y = b_ref[...] + w_ref[0:1, :] * ext[pl.ds(SUBLANES - lo, tr), :]
    for j in range(1, CONV_W):
        y = y + w_ref[j:j + 1, :] * ext[pl.ds(SUBLANES - lo + j, tr), :]
    if apply_silu:
        y = y * _sigmoid(y)
    o_ref[0] = y


def dwconv(p, c, w, bias, nctx_blk, apply_silu):
    b, tt, _ = p.shape
    tr = ROW_BLOCK
    nblk = tt // tr
    hb = tr // SUBLANES
    return pl.pallas_call(
        functools.partial(_conv_kernel, apply_silu, nctx_blk, nblk),
        grid=(b, nblk),
        in_specs=[
            pl.BlockSpec((1, SUBLANES, c), lambda bi, i: (bi, jnp.maximum(i * hb - 1, 0), 0)),
            pl.BlockSpec((1, tr, c), lambda bi, i: (bi, i, 0)),
            pl.BlockSpec((1, SUBLANES, c), lambda bi, i: (bi, jnp.minimum((i + 1) * hb, nblk * hb - 1), 0)),
            pl.BlockSpec((CONV_W, c), lambda bi, i: (0, 0)),
            pl.BlockSpec((1, c), lambda bi, i: (0, 0)),
        ],
        out_specs=pl.BlockSpec((1, tr, c), lambda bi, i: (bi, i, 0)),
        out_shape=jax.ShapeDtypeStruct((b, tt, c), f32),
        scratch_shapes=[pltpu.VMEM((tr + 2 * SUBLANES, c), f32)],
        compiler_params=_cparams(("arbitrary", "arbitrary")),
        name="dwconv",
    )(p, p, p, w, bias.reshape(1, c))


def _rglru_kernel(reverse, xc_ref, wa_ref, wx_ref, ba_ref, bx_ref, lam_ref, o_ref, carry):
    i = pl.program_id(1)

    @pl.when(i == 0)
    def _():
        carry[...] = jnp.zeros_like(carry)

    xc = xc_ref[0]
    tr = xc.shape[0]
    rs, gs = [], []
    for h in range(RG_HEADS):
        xh = xc[:, h * RG_HEAD_DIM:(h + 1) * RG_HEAD_DIM]
        rs.append(jnp.dot(xh, wa_ref[h], precision=HIGHEST, preferred_element_type=f32))
        gs.append(jnp.dot(xh, wx_ref[h], precision=HIGHEST, preferred_element_type=f32))
    r = _sigmoid(jnp.concatenate(rs, axis=1) + ba_ref[...])
    g = _sigmoid(jnp.concatenate(gs, axis=1) + bx_ref[...])
    log_a = -RG_C * r * _softplus(-lam_ref[...])
    a = jnp.exp(log_a)
    u = jnp.sqrt(-jnp.tanh(log_a) * (a * a + 1.0)) * (g * xc)
    rowid = lax.broadcasted_iota(i32, (tr, 1), 0)
    s = 1
    while s < tr:
        if reverse:
            keep = rowid < tr - s
            shift = tr - s
        else:
            keep = rowid >= s
            shift = s
        a_sh = jnp.where(keep, pltpu.roll(a, shift, 0), 1.0)
        u_sh = jnp.where(keep, pltpu.roll(u, shift, 0), 0.0)
        u = u + a * u_sh
        a = a * a_sh
        s *= 2
    hseq = a * carry[0:1, :] + u
    o_ref[0] = hseq
    last = 0 if reverse else tr - 1
    carry[0:1, :] = hseq[last:last + 1, :]


def rglru_scan(xc, wa, wx, ba, bx, lam, nctx_blk, reverse):
    b, tt, c = xc.shape
    tr = ROW_BLOCK
    nblk = tt // tr
    if reverse:
        def blk(bi, i):
            return (bi, jnp.where(i < nctx_blk, nctx_blk - 1 - i, nblk - 1 - (i - nctx_blk)), 0)
    else:
        def blk(bi, i):
            return (bi, i, 0)
    vec = pl.BlockSpec((1, c), lambda bi, i: (0, 0))
    wspec = pl.BlockSpec((RG_HEADS, RG_HEAD_DIM, RG_HEAD_DIM), lambda bi, i: (0, 0, 0))
    return pl.pallas_call(
        functools.partial(_rglru_kernel, reverse),
        grid=(b, nblk),
        in_specs=[pl.BlockSpec((1, tr, c), blk), wspec, wspec, vec, vec, vec],
        out_specs=pl.BlockSpec((1, tr, c), blk),
        out_shape=jax.ShapeDtypeStruct((b, tt, c), f32),
        scratch_shapes=[pltpu.VMEM((SUBLANES, c), f32)],
        compiler_params=_cparams(("arbitrary", "arbitrary")),
        name="rglru_rev" if reverse else "rglru_fwd",
    )(xc, wa, wx, ba.reshape(1, c), bx.reshape(1, c), lam.reshape(1, c))


def _dft_kernel(t_len, x_ref, o_ref, c2, s2):
    mi = pl.program_id(0)
    ki = pl.program_id(1)
    bm, bk = c2.shape
    w0 = 2.0 * math.pi / t_len

    @pl.when(ki == 0)
    def _():
        k = mi * bm + lax.broadcasted_iota(i32, (bm, bk), 0)
        a = lax.broadcasted_iota(i32, (bm, bk), 1)
        ang = ((k * a) & (t_len - 1)).astype(f32) * w0
        c2[...] = jnp.cos(ang)
        s2[...] = jnp.sin(ang)
        o_ref[...] = jnp.zeros_like(o_ref)

    kcol = mi * bm + lax.broadcasted_iota(i32, (bm, 1), 0)
    ang1 = ((kcol * (ki * bk)) & (t_len - 1)).astype(f32) * w0
    c1 = jnp.cos(ang1)
    s1 = jnp.sin(ang1)
    cm = (c1 * c2[...] - s1 * s2[...]).astype(bf16)
    sm = (s1 * c2[...] + c1 * s2[...]).astype(bf16)
    w = x_ref.shape[2]
    for b in range(x_ref.shape[0]):
        xb = x_ref[b].astype(bf16)
        o_ref[b, :, 0:w] += jnp.dot(cm, xb, preferred_element_type=f32)
        o_ref[b, :, w:2 * w] += jnp.dot(sm, xb, preferred_element_type=f32)


def dft_positions(p, col_blk, width, row0, t_len):
    b = p.shape[0]
    assert t_len & (t_len - 1) == 0
    bk = min(ROW_BLOCK, t_len)
    bm = min(1024, t_len)
    r0 = row0 // bk
    return pl.pallas_call(
        functools.partial(_dft_kernel, t_len),
        grid=(t_len // bm, t_len // bk),
        in_specs=[pl.BlockSpec((b, bk, width), lambda mi, ki: (0, r0 + ki, col_blk))],
        out_specs=pl.BlockSpec((b, bm, 2 * width), lambda mi, ki: (0, mi, 0)),
        out_shape=jax.ShapeDtypeStruct((b, t_len, 2 * width), f32),
        scratch_shapes=[pltpu.VMEM((bm, bk), f32), pltpu.VMEM((bm, bk), f32)],
        compiler_params=_cparams(("arbitrary", "arbitrary")),
        name="dft_positions",
    )(p)


def _residual_and_ffn_norm(h, y, mod_ref, gf_ref, hout_ref, f_ref):
    hn = h + mod_ref[0, 2:3, :] * y
    hout_ref[0] = hn
    f_ref[0] = _rms(hn, gf_ref[...]) * (1.0 + mod_ref[0, 4:5, :]) + mod_ref[0, 3:4, :]


def _merge_even_kernel(nctx_blk, sc_ctx, sc_lat, hf_ref, hb_ref, gate_ref, z_ref, h_ref, mod_ref, cc_ref, cs_ref,
                       fw_ref, fb_ref, wa_ref, wb_ref, gf_ref, hout_ref, f_ref):
    i = pl.program_id(1)
    ya = (hf_ref[0] + hb_ref[0]) * _gelu_tanh(gate_ref[0])
    z = z_ref[0]
    w = z.shape[1] // 2
    zz = (jnp.dot(z[:, :w], cc_ref[...], precision=HIGHEST, preferred_element_type=f32)
          - jnp.dot(z[:, w:], cs_ref[...], precision=HIGHEST, preferred_element_type=f32))
    zz = zz * jnp.where(i < nctx_blk, sc_ctx, sc_lat)
    yb = jnp.dot(zz, fw_ref[...], precision=HIGHEST, preferred_element_type=f32) + fb_ref[...]
    y = (jnp.dot(ya.astype(bf16), wa_ref[...], preferred_element_type=f32)
         + jnp.dot(yb.astype(bf16), wb_ref[...], preferred_element_type=f32))
    _residual_and_ffn_norm(h_ref[0], y, mod_ref, gf_ref, hout_ref, f_ref)


def merge_even(hf, hb, p, z, h, mod, fw_bd, fb, w_out, gf, nctx_blk, t_ctx, t_lat):
    b, tt, d = h.shape
    tr = ROW_BLOCK
    jj, kk = np.meshgrid(np.arange(FN_GROUP_DIM), np.arange(FN_GROUP_DIM), indexing="ij")
    ang = 2.0 * np.pi * jj * kk / FN_GROUP_DIM
    eye = np.eye(FN_GROUPS)
    cc = jnp.asarray(np.kron(eye, np.cos(ang)), f32)
    cs = jnp.asarray(np.kron(eye, np.sin(ang)), f32)
    sc_ctx = 1.0 / math.sqrt(t_ctx * FN_GROUP_DIM)
    sc_lat = 1.0 / math.sqrt(t_lat * FN_GROUP_DIM)

    def rows(wd, cb=0):
        return pl.BlockSpec((1, tr, wd), lambda bi, i: (bi, i, cb))

    def full(shape):
        return pl.BlockSpec(shape, lambda bi, i: (0,) * len(shape))

    return pl.pallas_call(
        functools.partial(_merge_even_kernel, nctx_blk, sc_ctx, sc_lat),
        grid=(b, tt // tr),
        in_specs=[
            rows(RG_WIDTH), rows(RG_WIDTH), rows(RG_WIDTH, 1), rows(2 * FN_WIDTH), rows(d),
            pl.BlockSpec((1, SUBLANES, d), lambda bi, i: (_mod_row(bi, i, nctx_blk), 0, 0)),
            full((FN_WIDTH, FN_WIDTH)), full((FN_WIDTH, FN_WIDTH)), full((FN_WIDTH, FN_WIDTH)), full((1, FN_WIDTH)),
            full((RG_WIDTH, d)), full((FN_WIDTH, d)), full((1, d)),
        ],
        out_specs=[rows(d), rows(d)],
        out_shape=[jax.ShapeDtypeStruct((b, tt, d), f32)] * 2,
        compiler_params=_cparams(("arbitrary", "arbitrary")),
        name="merge_even",
    )(hf, hb, p, z, h, mod, cc, cs, fw_bd, fb.reshape(1, FN_WIDTH),
      w_out[:RG_WIDTH].astype(bf16), w_out[RG_WIDTH:].astype(bf16), gf.reshape(1, d))


def _merge_odd_kernel(hs0_ref, hs1_ref, o_ref, at_ref, h_ref, mod_ref, hg_ref, wm_ref, wa_ref, gf_ref, hout_ref, f_ref):
    hs = hs0_ref[0, 0] + hs1_ref[0, 0]
    parts = []
    for hd in range(ML_HEADS):
        sl = slice(hd * ML_HEAD_DIM, (hd + 1) * ML_HEAD_DIM)
        parts.append(_rms(hs[:, sl], hg_ref[:, sl]))
    yml = _sigmoid(o_ref[0]) * jnp.concatenate(parts, axis=1)
    y = (jnp.dot(yml.astype(bf16), wm_ref[...], preferred_element_type=f32)
         + jnp.dot(at_ref[0].astype(bf16), wa_ref[...], preferred_element_type=f32))
    _residual_and_ffn_norm(h_ref[0], y, mod_ref, gf_ref, hout_ref, f_ref)


def merge_odd(hs, p, attn, h, mod, head_g, w_ml, w_at, gf, nctx_blk):
    b, t, ap = attn.shape
    d = h.shape[2]
    tr = ROW_BLOCK

    def rows(wd, cb=0, off=0):
        return pl.BlockSpec((1, tr, wd), lambda bi, i: (bi, i + off, cb))

    def full(shape):
        return pl.BlockSpec(shape, lambda bi, i: (0,) * len(shape))

    return pl.pallas_call(
        _merge_odd_kernel,
        grid=(b, t // tr),
        in_specs=[
            pl.BlockSpec((1, 1, tr, ML_WIDTH), lambda bi, i: (0, bi, i + nctx_blk, 0)),
            pl.BlockSpec((1, 1, tr, ML_WIDTH), lambda bi, i: (1, bi, i + nctx_blk, 0)),
            rows(ML_WIDTH, OD_O // ML_WIDTH, nctx_blk), rows(ap), rows(d, 0, nctx_blk),
            pl.BlockSpec((1, SUBLANES, d), lambda bi, i: (bi, 0, 0)),
            full((1, ML_WIDTH)), full((ML_WIDTH, d)), full((ap, d)), full((1, d)),
        ],
        out_specs=[rows(d), rows(d)],
        out_shape=[jax.ShapeDtypeStruct((b, t, d), f32)] * 2,
        compiler_params=_cparams(("arbitrary", "arbitrary")),
        name="merge_odd",
    )(hs, hs, p, attn, h, mod, head_g.reshape(1, ML_WIDTH), w_ml, w_at, gf.reshape(1, d))


def _mlstm_kernel(qk_ref, v_ref, g_ref, gt_ref, gb_ref, gbt_ref, o_ref, ct, nst, mst):
    d = pl.program_id(0)
    c = pl.program_id(2)
    L = ML_CHUNK

    @pl.when(c == 0)
    def _():
        ct[...] = jnp.zeros_like(ct)
        nst[...] = jnp.zeros_like(nst)
        mst[...] = jnp.zeros_like(mst)

    ri = lax.broadcasted_iota(i32, (L, L), 0)
    ci = lax.broadcasted_iota(i32, (L, L), 1)
    sgn = 1 - 2 * d
    tri = (ci - ri) * sgn <= 0
    trit = (ri - ci) * sgn <= 0
    gcol = g_ref[0] + gb_ref[...]
    grow = gt_ref[0] + gbt_ref[...]
    lane = lax.broadcasted_iota(i32, (1, LANES), 1)
    for hd in range(ML_HEADS):
        sl = slice(hd * ML_HEAD_DIM, (hd + 1) * ML_HEAD_DIM)
        q = qk_ref[0, :, sl] * (ML_HEAD_DIM ** -0.5)
        k = qk_ref[0, :, ML_WIDTH + hd * ML_HEAD_DIM:ML_WIDTH + (hd + 1) * ML_HEAD_DIM]
        v = v_ref[0, :, sl]
        li = d * 8 + hd
        ig_col = jnp.sum(jnp.where(lane == li, gcol, 0.0), axis=1, keepdims=True)
        fg_col = jnp.sum(jnp.where(lane == li + 4, gcol, 0.0), axis=1, keepdims=True)
        rsel = lax.broadcasted_iota(i32, (ML_N_GATES, 1), 0)
        ig_row = jnp.sum(jnp.where(rsel == li, grow, 0.0), axis=0, keepdims=True)
        fg_row = jnp.sum(jnp.where(rsel == li + 4, grow, 0.0), axis=0, keepdims=True)
        lf_col = -_softplus(-fg_col)
        lf_row = -_softplus(-fg_row)
        b_col = jnp.sum(jnp.where(tri, lf_row, 0.0), axis=1, keepdims=True)
        b_row = jnp.sum(jnp.where(trit, lf_col, 0.0), axis=0, keepdims=True)
        m_prev = mst[hd, 0:1, 0:1]
        n_prev = nst[hd, 0:1, :]
        ct_prev = ct[hd]
        dm = jnp.where(tri, b_col - b_row + ig_row, -jnp.inf)
        inter = b_col + m_prev
        m_row = jnp.maximum(jnp.max(dm, axis=1, keepdims=True), inter)
        qb, kb, vb = q.astype(bf16), k.astype(bf16), v.astype(bf16)
        s = lax.dot_general(qb, kb, (((1,), (1,)), ((), ())), preferred_element_type=f32) * jnp.exp(dm - m_row)
        w_inter = jnp.exp(inter - m_row)
        num = (jnp.dot(s.astype(bf16), vb, preferred_element_type=f32)
               + w_inter * jnp.dot(qb, ct_prev.astype(bf16), preferred_element_type=f32))
        den = jnp.sum(s, axis=1, keepdims=True) + w_inter * jnp.sum(q * n_prev, axis=1, keepdims=True)
        o_ref[0, 0, :, sl] = num / jnp.maximum(jnp.abs(den), jnp.exp(-m_row))
        b_l = jnp.sum(lf_col, axis=0, keepdims=True)
        log_wk = b_l - b_col + ig_col
        m_new = jnp.maximum(b_l + m_prev, jnp.max(log_wk, axis=0, keepdims=True))
        kw = jnp.exp(log_wk - m_new) * k
        decay = jnp.exp(b_l + m_prev - m_new)
        ct[hd] = decay * ct_prev + lax.dot_general(kw.astype(bf16), vb, (((0,), (0,)), ((), ())),
                                                   preferred_element_type=f32)
        nst[hd, 0:1, :] = decay * n_prev + jnp.sum(kw, axis=0, keepdims=True)
        mst[hd] = jnp.broadcast_to(m_new, (SUBLANES, LANES))


def mlstm(qkc, p, gt, gate_b, nctx_chunks):
    b, tt, _ = qkc.shape
    L = ML_CHUNK
    nch = tt // L

    def chunk(d, c):
        return jnp.where(d == 0, c, jnp.where(c < nctx_chunks, nctx_chunks - 1 - c, nch - 1 - (c - nctx_chunks)))

    gb = jnp.zeros((1, LANES), f32).at[0, :ML_N_GATES].set(gate_b)
    return pl.pallas_call(
        _mlstm_kernel,
        grid=(2, b, nch),
        in_specs=[
            pl.BlockSpec((1, L, 2 * ML_WIDTH), lambda d, bi, c: (bi, chunk(d, c), 0)),
            pl.BlockSpec((1, L, ML_WIDTH), lambda d, bi, c: (bi, chunk(d, c), OD_V // ML_WIDTH)),
            pl.BlockSpec((1, L, LANES), lambda d, bi, c: (bi, chunk(d, c), OD_G // LANES)),
            pl.BlockSpec((1, ML_N_GATES, L), lambda d, bi, c: (bi, 0, chunk(d, c))),
            pl.BlockSpec((1, LANES), lambda d, bi, c: (0, 0)),
            pl.BlockSpec((ML_N_GATES, 1), lambda d, bi, c: (0, 0)),
        ],
        out_specs=pl.BlockSpec((1, 1, L, ML_WIDTH), lambda d, bi, c: (d, bi, chunk(d, c), 0)),
        out_shape=jax.ShapeDtypeStruct((2, b, tt, ML_WIDTH), f32),
        scratch_shapes=[
            pltpu.VMEM((ML_HEADS, ML_HEAD_DIM, ML_HEAD_DIM), f32),
            pltpu.VMEM((ML_HEADS, SUBLANES, LANES), f32),
            pltpu.VMEM((ML_HEADS, SUBLANES, LANES), f32),
        ],
        compiler_params=_cparams(("arbitrary", "arbitrary", "arbitrary")),
        name="mlstm",
    )(qkc, p, p, gt, gb, gate_b.reshape(ML_N_GATES, 1))


def _mla_prep_kernel(nctx_blk, p_ref, qg_ref, kg_ref, wq_ref, wqs_ref, wk_ref, wv_ref, em_ref, es_ref,
                     q_ref, k_ref, v_ref):
    i = pl.program_id(1)
    tr = p_ref.shape[1]
    cq = _rms(p_ref[0, :, OD_CQ:OD_CKV], qg_ref[...]).astype(bf16)
    ckv = _rms(p_ref[0, :, OD_CKV:OD_KR], kg_ref[...]).astype(bf16)
    kr = p_ref[0, :, OD_KR:OD_N]
    lane = lax.broadcasted_iota(i32, (1, MLA_HEAD_PAD), 1)
    half = MLA_ROPE // 2
    pair = jnp.where(lane < MLA_NOPE + half, lane - MLA_NOPE, lane - MLA_NOPE - half)
    is_rope = jnp.logical_and(lane >= MLA_NOPE, lane < MLA_NOPE + MLA_ROPE)
    use_row = pair < half // 2
    fidx = jnp.clip(jnp.where(use_row, pair, pair - half // 2), 0, half // 2 - 1).astype(f32)
    inv = jnp.exp(fidx * (-math.log(ROPE_THETA) / (half // 2)))
    t = (i - nctx_blk) * tr + lax.broadcasted_iota(i32, (tr, 1), 0)
    pos = jnp.where(use_row, (t >> GRID_SHIFT).astype(f32), (t & (GRID_W - 1)).astype(f32))
    ang = pos * inv
    rot = jnp.logical_and(is_rope, i >= nctx_blk)
    cos1 = jnp.where(rot, jnp.cos(ang), 1.0)
    sin1 = jnp.where(rot, jnp.sin(ang), 0.0)
    cos_t = jnp.concatenate([cos1] * MLA_HEADS, axis=1)
    sin_t = jnp.concatenate([sin1] * MLA_HEADS, axis=1)
    scale = (MLA_NOPE + MLA_ROPE) ** -0.5
    q = (jnp.dot(cq, wq_ref[...], preferred_element_type=f32) * cos_t
         + jnp.dot(cq, wqs_ref[...], preferred_element_type=f32) * sin_t)
    q_ref[0] = (q * scale).astype(bf16)
    k = (jnp.dot(ckv, wk_ref[...], preferred_element_type=f32)
         + jnp.dot(kr, em_ref[...], precision=HIGHEST, preferred_element_type=f32) * cos_t
         + jnp.dot(kr, es_ref[...], precision=HIGHEST, preferred_element_type=f32) * sin_t)
    k_ref[0] = k.astype(bf16)
    v_ref[0] = jnp.dot(ckv, wv_ref[...], preferred_element_type=f32).astype(bf16)


def _mla_weights(w_uq, w_ukv):
    hp = MLA_HEAD_PAD
    half = MLA_ROPE // 2
    wq = w_uq.reshape(MLA_Q_LORA, MLA_HEADS, MLA_NOPE + MLA_ROPE)
    nope, x1, x2 = wq[..., :MLA_NOPE], wq[..., MLA_NOPE::2], wq[..., MLA_NOPE + 1::2]
    zq = jnp.zeros((MLA_Q_LORA, MLA_HEADS, hp - MLA_NOPE - MLA_ROPE), f32)
    wq_main = jnp.concatenate([nope, x1, x2, zq], axis=-1).reshape(MLA_Q_LORA, MLA_HEADS * hp)
    wq_swap = jnp.concatenate([jnp.zeros_like(nope), -x2, x1, zq], axis=-1).reshape(MLA_Q_LORA, MLA_HEADS * hp)
    wkv = w_ukv.reshape(MLA_KV_LORA, MLA_HEADS, MLA_NOPE + MLA_V)
    zk = jnp.zeros((MLA_KV_LORA, MLA_HEADS, hp - MLA_NOPE), f32)
    wk = jnp.concatenate([wkv[..., :MLA_NOPE], zk], axis=-1).reshape(MLA_KV_LORA, MLA_HEADS * hp)
    wv = jnp.concatenate([wkv[..., MLA_NOPE:], jnp.zeros((MLA_KV_LORA, MLA_HEADS, hp - MLA_V), f32)],
                         axis=-1).reshape(MLA_KV_LORA, MLA_HEADS * hp)
    em = np.zeros((LANES, MLA_HEADS, hp), np.float32)
    es = np.zeros((LANES, MLA_HEADS, hp), np.float32)
    for j in range(half):
        em[2 * j, :, MLA_NOPE + j] = 1.0
        em[2 * j + 1, :, MLA_NOPE + half + j] = 1.0
        es[2 * j + 1, :, MLA_NOPE + j] = -1.0
        es[2 * j, :, MLA_NOPE + half + j] = 1.0
    em = jnp.asarray(em.reshape(LANES, MLA_HEADS * hp))
    es = jnp.asarray(es.reshape(LANES, MLA_HEADS * hp))
    return wq_main.astype(bf16), wq_swap.astype(bf16), wk.astype(bf16), wv.astype(bf16), em, es


def mla_prep(p, q_norm_g, kv_norm_g, w_uq, w_ukv, nctx_blk):
    b, tt, n = p.shape
    tr = ROW_BLOCK
    hw = MLA_HEADS * MLA_HEAD_PAD
    wq, wqs, wk, wv, em, es = _mla_weights(w_uq, w_ukv)

    def full(shape):
        return pl.BlockSpec(shape, lambda bi, i: (0,) * len(shape))

    out = pl.BlockSpec((1, tr, hw), lambda bi, i: (bi, i, 0))
    return pl.pallas_call(
        functools.partial(_mla_prep_kernel, nctx_blk),
        grid=(b, tt // tr),
        in_specs=[pl.BlockSpec((1, tr, n), lambda bi, i: (bi, i, 0)),
                  full((1, MLA_Q_LORA)), full((1, MLA_KV_LORA)),
                  full((MLA_Q_LORA, hw)), full((MLA_Q_LORA, hw)), full((MLA_KV_LORA, hw)), full((MLA_KV_LORA, hw)),
                  full((LANES, hw)), full((LANES, hw))],
        out_specs=[out, out, out],
        out_shape=[jax.ShapeDtypeStruct((b, tt, hw), bf16)] * 3,
        compiler_params=_cparams(("arbitrary", "arbitrary")),
        name="mla_prep",
    )(p, q_norm_g.reshape(1, MLA_Q_LORA), kv_norm_g.reshape(1, MLA_KV_LORA), wq, wqs, wk, wv, em, es)


def _mla_attn_kernel(q_ref, k_ref, v_ref, o_ref):
    s = lax.dot_general(q_ref[0], k_ref[0], (((1,), (1,)), ((), ())), preferred_element_type=f32)
    m = jnp.max(s, axis=-1, keepdims=True)
    e = jnp.exp(s - m)
    l = jnp.sum(e, axis=-1, keepdims=True)
    o_ref[0] = jnp.dot(e.astype(bf16), v_ref[0], preferred_element_type=f32) / l


def mla_attention(q, k, v, nctx_blk):
    b, tt, hw = q.shape
    tq = ROW_BLOCK
    nq = tt // tq - nctx_blk
    hp = MLA_HEAD_PAD
    kv = pl.BlockSpec((1, tt, hp), lambda bi, h, qi: (bi, 0, h))
    return pl.pallas_call(
        _mla_attn_kernel,
        grid=(b, MLA_HEADS, nq),
        in_specs=[pl.BlockSpec((1, tq, hp), lambda bi, h, qi: (bi, qi + nctx_blk, h)), kv, kv],
        out_specs=pl.BlockSpec((1, tq, hp), lambda bi, h, qi: (bi, qi, h)),
        out_shape=jax.ShapeDtypeStruct((b, nq * tq, hw), f32),
        compiler_params=_cparams(("arbitrary", "arbitrary", "arbitrary")),
        name="mla_attention",
    )(q, k, v)


def _topk_rows(s, k, payload=None):
    n = s.shape[0]
    rid = lax.broadcasted_iota(i32, s.shape, 0)
    vals, outs = [], []
    for _ in range(k):
        m = jnp.max(s, axis=0, keepdims=True)
        am = jnp.min(jnp.where(s == m, rid, n), axis=0, keepdims=True)
        hit = rid == am
        vals.append(m)
        outs.append(am if payload is None else jnp.sum(jnp.where(hit, payload, 0), axis=0, keepdims=True))
        s = jnp.where(hit, -jnp.inf, s)
    return jnp.concatenate(vals, axis=0), jnp.concatenate(outs, axis=0)


def _peer_route_kernel(f_ref, wq_ref, keys_ref, m8_ref, sh_ref, g_ref):
    q = jnp.dot(f_ref[...].astype(bf16), wq_ref[...], preferred_element_type=f32)
    hk = PEER_KEY_DIM // 2
    for h in range(PEER_HEADS):
        vals, idxs = [], []
        for p in range(2):
            c0 = (h * 2 + p) * hk
            qhp = q[:, c0:c0 + hk].astype(bf16)
            st = lax.dot_general(keys_ref[p, h], qhp, (((1,), (1,)), ((), ())), preferred_element_type=f32)
            v, ix = _topk_rows(st, PEER_TOPK)
            vals.append(v)
            idxs.append(ix)
        cand = jnp.concatenate([vals[0][a:a + 1, :] + vals[1] for a in range(PEER_TOPK)], axis=0)
        cidx = jnp.concatenate([idxs[0][a:a + 1, :] * PEER_N_KEYS + idxs[1] for a in range(PEER_TOPK)], axis=0)
        top_s, e = _topk_rows(cand, PEER_TOPK, payload=cidx)
        ex = jnp.exp(top_s - top_s[0:1, :])
        rows = slice(h * PEER_TOPK, (h + 1) * PEER_TOPK)
        g_ref[rows, :] = ex / jnp.sum(ex, axis=0, keepdims=True)
        m8_ref[rows, :] = (e >> 1) * SUBLANES
        sh_ref[rows, :] = (1 - (e & 1)) * 16


def peer_route(f2, wq, keys):
    r, d = f2.shape
    tb = PEER_TB
    out = pl.BlockSpec((PEER_PICKS, tb), lambda i: (0, i))
    return pl.pallas_call(
        _peer_route_kernel,
        grid=(r // tb,),
        in_specs=[
            pl.BlockSpec((tb, d), lambda i: (i, 0)),
            pl.BlockSpec(wq.shape, lambda i: (0, 0)),
            pl.BlockSpec(keys.shape, lambda i: (0, 0, 0, 0)),
        ],
        out_specs=[out, out, out],
        out_shape=[jax.ShapeDtypeStruct((PEER_PICKS, r), i32), jax.ShapeDtypeStruct((PEER_PICKS, r), i32),
                   jax.ShapeDtypeStruct((PEER_PICKS, r), f32)],
        compiler_params=_cparams(("arbitrary",)),
        name="peer_route",
    )(f2, wq, keys)


def pack_expert_table(t):
    e, d = t.shape
    bits = lax.bitcast_convert_type(t.astype(bf16), jnp.uint16).astype(jnp.uint32)
    bits = bits.reshape(e // 2, 2, SUBLANES, d // SUBLANES)
    word = bits[:, 0] | (bits[:, 1] << 16)
    return lax.bitcast_convert_type(word, i32).reshape(e // 2 * SUBLANES, d // SUBLANES)


def _expert_row(tab_ref, m8, sh):
    word = tab_ref[pl.ds(pl.multiple_of(m8, SUBLANES), SUBLANES), :]
    return pltpu.bitcast((word << sh) & HI_MASK, f32)


def _sublane_sums(ps, row):
    lo4 = row < 4
    t = []
    for s in range(4):
        a, b = ps[s], ps[s + 4]
        t.append(jnp.where(lo4, a, b) + pltpu.roll(jnp.where(lo4, b, a), 4, 0))
    m2 = (row & 2) == 0
    u = []
    for s in range(2):
        a, b = t[s], t[s + 2]
        u.append(jnp.where(m2, a, b) + jnp.where(m2, pltpu.roll(a, 6, 0), pltpu.roll(b, 2, 0)))
    m1 = (row & 1) == 0
    a, b = u[0], u[1]
    return jnp.where(m1, a, b) + jnp.where(m1, pltpu.roll(a, 7, 0), pltpu.roll(b, 1, 0))


def _load_table_once(tab_hbm, tab_vmem, sem):
    @pl.when(pl.program_id(0) == 0)
    def _():
        cp = pltpu.make_async_copy(tab_hbm, tab_vmem, sem)
        cp.start()
        cp.wait()


def _fetch_rows(src_hbm, dst_smem, sem, i, tb):
    cp = pltpu.make_async_copy(src_hbm.at[pl.ds(i * tb, tb)], dst_smem, sem)
    cp.start()
    return cp


def _peer_act_kernel(m8_hbm, sh_hbm, x_ref, g_ref, tab_hbm, w_ref, tab_vmem, m8_s, sh_s, rbuf, sem_tab, sem_a, sem_b):
    i = pl.program_id(0)
    tb = x_ref.shape[0]
    _load_table_once(tab_hbm, tab_vmem, sem_tab)
    ca = _fetch_rows(m8_hbm, m8_s, sem_a, i, tb)
    cb = _fetch_rows(sh_hbm, sh_s, sem_b, i, tb)
    ca.wait()
    cb.wait()
    row = lax.broadcasted_iota(i32, (SUBLANES, LANES), 0)
    ng = PEER_PICKS // SUBLANES

    def tok(t, c):
        x = x_ref[t]

        def grp(jg, c2):
            j0 = jg * SUBLANES
            ps = [_expert_row(tab_vmem, m8_s[t, j0 + s], sh_s[t, j0 + s]) * x for s in range(SUBLANES)]
            rbuf[t * ng + jg] = _sublane_sums(ps, row)
            return c2

        return lax.fori_loop(0, ng, grp, c)

    lax.fori_loop(0, tb, tok, 0)
    act = jnp.sum(rbuf[...].reshape(tb, PEER_PICKS, LANES), axis=-1)
    w_ref[...] = g_ref[...] * _gelu_tanh(act)


def peer_act(m8, sh, x3, g, tab):
    r = m8.shape[0]
    tb = PEER_TB
    anyspec = pl.BlockSpec(memory_space=pl.ANY)
    return pl.pallas_call(
        _peer_act_kernel,
        grid=(r // tb,),
        in_specs=[anyspec, anyspec,
                  pl.BlockSpec((tb, SUBLANES, LANES), lambda i: (i, 0, 0)),
                  pl.BlockSpec((tb, PEER_PICKS), lambda i: (i, 0)),
                  anyspec],
        out_specs=pl.BlockSpec((tb, PEER_PICKS), lambda i: (i, 0)),
        out_shape=jax.ShapeDtypeStruct((r, PEER_PICKS), f32),
        scratch_shapes=[
            pltpu.VMEM(tab.shape, i32),
            pltpu.SMEM((tb, PEER_PICKS), i32),
            pltpu.SMEM((tb, PEER_PICKS), i32),
            pltpu.VMEM((tb * PEER_PICKS // SUBLANES, SUBLANES, LANES), f32),
            pltpu.SemaphoreType.DMA, pltpu.SemaphoreType.DMA, pltpu.SemaphoreType.DMA,
        ],
        compiler_params=_cparams(("arbitrary",), VMEM_LIMIT_TABLE),
        name="peer_act",
    )(m8, sh, x3, g, tab)


def _peer_out_kernel(m8_hbm, sh_hbm, w_hbm, tab_hbm, o_ref, tab_vmem, m8_s, sh_s, w_s, sem_tab, sem_a, sem_b, sem_c):
    i = pl.program_id(0)
    tb = o_ref.shape[0]
    _load_table_once(tab_hbm, tab_vmem, sem_tab)
    ca = _fetch_rows(m8_hbm, m8_s, sem_a, i, tb)
    cb = _fetch_rows(sh_hbm, sh_s, sem_b, i, tb)
    cc = _fetch_rows(w_hbm, w_s, sem_c, i, tb)
    ca.wait()
    cb.wait()
    cc.wait()
    nacc = 4

    def tok(t, c):
        def grp(jg, accs):
            j0 = jg * SUBLANES
            accs = list(accs)
            for s in range(SUBLANES):
                accs[s % nacc] = accs[s % nacc] + w_s[t, j0 + s] * _expert_row(tab_vmem, m8_s[t, j0 + s], sh_s[t, j0 + s])
            return tuple(accs)

        z = jnp.zeros((SUBLANES, LANES), f32)
        accs = lax.fori_loop(0, PEER_PICKS // SUBLANES, grp, (z,) * nacc)
        o_ref[t] = (accs[0] + accs[1]) + (accs[2] + accs[3])
        return c

    lax.fori_loop(0, tb, tok, 0)


def peer_out(m8, sh, w, tab):
    r = m8.shape[0]
    tb = PEER_TB
    anyspec = pl.BlockSpec(memory_space=pl.ANY)
    return pl.pallas_call(
        _peer_out_kernel,
        grid=(r // tb,),
        in_specs=[anyspec, anyspec, anyspec, anyspec],
        out_specs=pl.BlockSpec((tb, SUBLANES, LANES), lambda i: (i, 0, 0)),
        out_shape=jax.ShapeDtypeStruct((r, SUBLANES, LANES), f32),
        scratch_shapes=[
            pltpu.VMEM(tab.shape, i32),
            pltpu.SMEM((tb, PEER_PICKS), i32),
            pltpu.SMEM((tb, PEER_PICKS), i32),
            pltpu.SMEM((tb, PEER_PICKS), f32),
            pltpu.SemaphoreType.DMA, pltpu.SemaphoreType.DMA, pltpu.SemaphoreType.DMA, pltpu.SemaphoreType.DMA,
        ],
        compiler_params=_cparams(("arbitrary",), VMEM_LIMIT_TABLE),
        name="peer_out",
    )(m8, sh, w, tab)


def peer_ffn(f, wq, keys, u, v):
    b, r, d = f.shape
    f2 = f.reshape(b * r, d)
    m8t, sht, gt = peer_route(f2, wq.astype(bf16), keys.astype(bf16))
    m8, sh, g = m8t.T, sht.T, gt.T
    w = peer_act(m8, sh, f2.reshape(b * r, SUBLANES, d // SUBLANES), g, pack_expert_table(u))
    o = peer_out(m8, sh, w, pack_expert_table(v))
    return o.reshape(b, r, d)


def _final_kernel(h_ref, peer_ref, mod_ref, g_ref, o_ref):
    h = h_ref[0] + mod_ref[0, 5:6, :] * peer_ref[0]
    o_ref[0] = _rms(h, g_ref[...])


def final_norm(h, peer, mod, g):
    b, t, d = h.shape
    tr = ROW_BLOCK
    row = pl.BlockSpec((1, tr, d), lambda bi, i: (bi, i, 0))
    return pl.pallas_call(
        _final_kernel,
        grid=(b, t // tr),
        in_specs=[row, row, pl.BlockSpec((1, SUBLANES, d), lambda bi, i: (bi, 0, 0)),
                  pl.BlockSpec((1, d), lambda bi, i: (0, 0))],
        out_specs=row,
        out_shape=jax.ShapeDtypeStruct((b, t, d), f32),
        compiler_params=_cparams(("arbitrary", "arbitrary")),
        name="final_norm",
    )(h, peer, mod, g.reshape(1, d))


def _block_diag(w):
    g, a, c = w.shape
    out = jnp.zeros((g * a, g * c), w.dtype)
    for k in range(g):
        out = out.at[k * a:(k + 1) * a, k * c:(k + 1) * c].set(w[k])
    return out


def kernel(x, c, ctx, c_ctx, mod_w, mod_b, norm_mix_g, norm_ffn_g, ev_w_in, ev_w_out, rg_conv_w, rg_conv_b, rg_wa, rg_ba, rg_wx, rg_bx, rg_lam, fn_w, fn_b, od_w_in, od_w_out, ml_conv_w, ml_conv_b, ml_gate_b, ml_head_g, mla_q_norm_g, mla_w_uq, mla_kv_norm_g, mla_w_ukv, peer_w_q, peer_keys, peer_u, peer_v, final_g):
    b, t, d = x.shape
    tc = ctx.shape[1]
    assert b == 2 and tc % ROW_BLOCK == 0 and t % ROW_BLOCK == 0 and mod_w.shape[0] == 2
    nctx_blk = tc // ROW_BLOCK

    cvec = jnp.zeros((SUBLANES, d), f32).at[0:b].set(c).at[b].set(c_ctx)
    mods = modulation(cvec, mod_w, mod_b)
    mods = mods.reshape(2, SUBLANES, 6, d)[:, :3]
    mods = jnp.pad(mods, ((0, 0), (0, 0), (0, 2), (0, 0)))

    h0 = jnp.concatenate([ctx, x], axis=1)

    p = inproj(h0, mods[0], norm_mix_g[0], ev_w_in[0].astype(bf16), nctx_blk)
    xc = dwconv(p, RG_WIDTH, rg_conv_w[0], rg_conv_b[0], nctx_blk, apply_silu=False)
    hf = rglru_scan(xc, rg_wa[0, 0], rg_wx[0, 0], rg_ba[0, 0], rg_bx[0, 0], rg_lam[0, 0], nctx_blk, reverse=False)
    hb = rglru_scan(xc, rg_wa[0, 1], rg_wx[0, 1], rg_ba[0, 1], rg_bx[0, 1], rg_lam[0, 1], nctx_blk, reverse=True)
    fcol = 2 * RG_WIDTH // FN_WIDTH
    z = jnp.concatenate([dft_positions(p, fcol, FN_WIDTH, 0, tc), dft_positions(p, fcol, FN_WIDTH, tc, t)], axis=1)
    h1, f1 = merge_even(hf, hb, p, z, h0, mods[0], _block_diag(fn_w[0]), fn_b[0], ev_w_out[0], norm_ffn_g[0],
                        nctx_blk, tc, t)
    peer1 = peer_ffn(f1, peer_w_q[0], peer_keys[0], peer_u[0], peer_v[0])

    w = od_w_in[0]
    zpad = lambda n: jnp.zeros((d, n), f32)
    w_od = jnp.concatenate([w[:, :OD_G + ML_N_GATES], zpad(OD_CQ - OD_G - ML_N_GATES),
                            w[:, 2064:2064 + MLA_Q_LORA + MLA_KV_LORA + MLA_ROPE],
                            zpad(OD_N - OD_KR - MLA_ROPE)], axis=1).astype(bf16)
    h2, p2 = inproj(h1, mods[1], norm_mix_g[1], w_od, nctx_blk, peer=peer1, mod_prev=mods[0])
    qkc = dwconv(p2, 2 * ML_WIDTH, ml_conv_w[0], ml_conv_b[0], nctx_blk, apply_silu=True)
    gt = jnp.swapaxes(p2[:, :, OD_G:OD_G + ML_N_GATES], 1, 2)
    hs = mlstm(qkc, p2, gt, ml_gate_b[0], tc // ML_CHUNK)
    q, k, v = mla_prep(p2, mla_q_norm_g[0], mla_kv_norm_g[0], mla_w_uq[0], mla_w_ukv[0], nctx_blk)
    attn = mla_attention(q, k, v, nctx_blk)
    wo = od_w_out[0]
    w_at = jnp.concatenate([wo[ML_WIDTH:].reshape(MLA_HEADS, MLA_V, d),
                            jnp.zeros((MLA_HEADS, MLA_HEAD_PAD - MLA_V, d), f32)], axis=1)
    w_at = w_at.reshape(MLA_HEADS * MLA_HEAD_PAD, d).astype(bf16)
    h3, f3 = merge_odd(hs, p2, attn, h2, mods[1, :b], ml_head_g[0], wo[:ML_WIDTH].astype(bf16), w_at,
                       norm_ffn_g[1], nctx_blk)
    peer3 = peer_ffn(f3, peer_w_q[1], peer_keys[1], peer_u[1], peer_v[1])
    return final_norm(h3, peer3, mods[1, :b], final_g)
```

```python
import functools
import math

import numpy as np
import jax
import jax.numpy as jnp
from jax import lax
from jax.experimental import pallas as pl
from jax.experimental.pallas import tpu as pltpu

f32 = jnp.float32
bf16 = jnp.bfloat16
i32 = jnp.int32
HIGHEST = lax.Precision.HIGHEST

SUBLANES = 8
LANES = 128
VMEM_LIMIT = 48 * 1024 * 1024
VMEM_LIMIT_TABLE = 56 * 1024 * 1024

EPS = 1e-6
GRID_W = 64
GRID_SHIFT = 6
ROW_BLOCK = 256
CONV_W = 4
RG_HEADS, RG_HEAD_DIM = 6, 128
RG_WIDTH = RG_HEADS * RG_HEAD_DIM
RG_C = 8.0
FN_GROUPS, FN_GROUP_DIM = 4, 64
FN_WIDTH = FN_GROUPS * FN_GROUP_DIM
ML_HEADS, ML_HEAD_DIM = 4, 128
ML_WIDTH = ML_HEADS * ML_HEAD_DIM
ML_CHUNK = 128
ML_N_GATES = 2 * 2 * ML_HEADS
MLA_HEADS, MLA_NOPE, MLA_ROPE, MLA_V = 8, 64, 32, 64
MLA_Q_LORA, MLA_KV_LORA = 384, 256
MLA_HEAD_PAD = 128
ROPE_THETA = 10000.0
PEER_HEADS, PEER_KEY_DIM, PEER_N_KEYS, PEER_TOPK = 8, 256, 128, 16
PEER_PICKS = PEER_HEADS * PEER_TOPK
PEER_TB = 128
PEER_GROUP_UNROLL = 8
HI_MASK = -65536
ROW_MASK = 0xFFF8

OD_V, OD_O, OD_G, OD_CQ, OD_CKV, OD_KR, OD_N = 1024, 1536, 2048, 2176, 2560, 2816, 2944


def _cparams(sem, vmem=VMEM_LIMIT):
    return pltpu.CompilerParams(dimension_semantics=sem, vmem_limit_bytes=vmem)


def _gelu_tanh(x):
    return 0.5 * x * (1.0 + jnp.tanh(0.7978845608028654 * (x + 0.044715 * x * x * x)))


def _sigmoid(x):
    return 1.0 / (1.0 + jnp.exp(-x))


def _softplus(x):
    return jnp.maximum(x, 0.0) + jnp.log1p(jnp.exp(-jnp.abs(x)))


def _rms(x, g):
    return x * lax.rsqrt(jnp.mean(x * x, axis=-1, keepdims=True) + EPS) * g


def _mod_row(b, i, nctx_blk):
    return jnp.where(i < nctx_blk, 2, b)


def _mod_kernel(c_ref, w_ref, b_ref, o_ref):
    c = c_ref[...]
    s = c * _sigmoid(c)
    o_ref[0] = jnp.dot(s, w_ref[0], precision=HIGHEST, preferred_element_type=f32) + b_ref[0]


def modulation(cvec, mod_w, mod_b):
    depth, d, n = mod_w.shape
    tn = 1536
    return pl.pallas_call(
        _mod_kernel,
        grid=(depth, n // tn),
        in_specs=[
            pl.BlockSpec((SUBLANES, d), lambda l, j: (0, 0)),
            pl.BlockSpec((1, d, tn), lambda l, j: (l, 0, j)),
            pl.BlockSpec((1, 1, tn), lambda l, j: (l, 0, j)),
        ],
        out_specs=pl.BlockSpec((1, SUBLANES, tn), lambda l, j: (l, 0, j)),
        out_shape=jax.ShapeDtypeStruct((depth, SUBLANES, n), f32),
        compiler_params=_cparams(("arbitrary", "arbitrary")),
        name="modulation",
    )(cvec, mod_w, mod_b.reshape(depth, 1, n))


def _inproj_kernel(has_res, *refs):
    if has_res:
        h_ref, peer_ref, modp_ref, mod_ref, g_ref, w_ref, hout_ref, p_ref = refs
        h = h_ref[0] + modp_ref[0, 5:6, :] * peer_ref[0]
        hout_ref[0] = h
    else:
        h_ref, mod_ref, g_ref, w_ref, p_ref = refs
        h = h_ref[0]
    u = _rms(h, g_ref[...]) * (1.0 + mod_ref[0, 1:2, :]) + mod_ref[0, 0:1, :]
    p_ref[0] = jnp.dot(u.astype(bf16), w_ref[...], preferred_element_type=f32)


def inproj(h, mod, g, w, nctx_blk, peer=None, mod_prev=None):
    b, tt, d = h.shape
    n = w.shape[1]
    tr = ROW_BLOCK
    has_res = peer is not None
    row = pl.BlockSpec((1, tr, d), lambda bi, i: (bi, i, 0))
    modspec = pl.BlockSpec((1, SUBLANES, d), lambda bi, i: (_mod_row(bi, i, nctx_blk), 0, 0))
    in_specs = [row] + ([row, modspec] if has_res else []) + [
        modspec,
        pl.BlockSpec((1, d), lambda bi, i: (0, 0)),
        pl.BlockSpec((d, n), lambda bi, i: (0, 0)),
    ]
    pspec = pl.BlockSpec((1, tr, n), lambda bi, i: (bi, i, 0))
    pshape = jax.ShapeDtypeStruct((b, tt, n), f32)
    args = [h] + ([peer, mod_prev] if has_res else []) + [mod, g.reshape(1, d), w]
    return pl.pallas_call(
        functools.partial(_inproj_kernel, has_res),
        grid=(b, tt // tr),
        in_specs=in_specs,
        out_specs=[row, pspec] if has_res else pspec,
        out_shape=[jax.ShapeDtypeStruct((b, tt, d), f32), pshape] if has_res else pshape,
        compiler_params=_cparams(("arbitrary", "arbitrary")),
        name="inproj",
    )(*args)


def _conv_kernel(apply_silu, nctx_blk, nblk, prev_ref, cur_ref, next_ref, w_ref, b_ref, o_ref, ext):
    i = pl.program_id(1)
    tr = cur_ref.shape[1]
    has_prev = jnp.logical_and(i != 0, i != nctx_blk)
    has_next = jnp.logical_and(i != nctx_blk - 1, i != nblk - 1)
    ext[0:SUBLANES] = jnp.where(has_prev, prev_ref[0], 0.0)
    ext[SUBLANES:SUBLANES + tr] = cur_ref[0]
    ext[SUBLANES + tr:2 * SUBLANES + tr] = jnp.where(has_next, next_ref[0], 0.0)
    lo = CONV_W // 2
    y = b_ref[...] + w_ref[0:1, :] * ext[pl.ds(SUBLANES - lo, tr), :]
    for j in range(1, CONV_W):
        y = y + w_ref[j:j + 1, :] * ext[pl.ds(SUBLANES - lo + j, tr), :]
    if apply_silu:
        y = y * _sigmoid(y)
    o_ref[0] = y


def dwconv(p, c, w, bias, nctx_blk, apply_silu):
    b, tt, _ = p.shape
    tr = ROW_BLOCK
    nblk = tt // tr
    hb = tr // SUBLANES
    return pl.pallas_call(
        functools.partial(_conv_kernel, apply_silu, nctx_blk, nblk),
        grid=(b, nblk),
        in_specs=[
            pl.BlockSpec((1, SUBLANES, c), lambda bi, i: (bi, jnp.maximum(i * hb - 1, 0), 0)),
            pl.BlockSpec((1, tr, c), lambda bi, i: (bi, i, 0)),
            pl.BlockSpec((1, SUBLANES, c), lambda bi, i: (bi, jnp.minimum((i + 1) * hb, nblk * hb - 1), 0)),
            pl.BlockSpec((CONV_W, c), lambda bi, i: (0, 0)),
            pl.BlockSpec((1, c), lambda bi, i: (0, 0)),
        ],
        out_specs=pl.BlockSpec((1, tr, c), lambda bi, i: (bi, i, 0)),
        out_shape=jax.ShapeDtypeStruct((b, tt, c), f32),
        scratch_shapes=[pltpu.VMEM((tr + 2 * SUBLANES, c), f32)],
        compiler_params=_cparams(("arbitrary", "arbitrary")),
        name="dwconv",
    )(p, p, p, w, bias.reshape(1, c))


def _rglru_kernel(reverse, xc_ref, wa_ref, wx_ref, ba_ref, bx_ref, lam_ref, o_ref, carry):
    i = pl.program_id(1)

    @pl.when(i == 0)
    def _():
        carry[...] = jnp.zeros_like(carry)

    xc = xc_ref[0]
    tr = xc.shape[0]
    rs, gs = [], []
    for h in range(RG_HEADS):
        xh = xc[:, h * RG_HEAD_DIM:(h + 1) * RG_HEAD_DIM]
        rs.append(jnp.dot(xh, wa_ref[h], precision=HIGHEST, preferred_element_type=f32))
        gs.append(jnp.dot(xh, wx_ref[h], precision=HIGHEST, preferred_element_type=f32))
    r = _sigmoid(jnp.concatenate(rs, axis=1) + ba_ref[...])
    g = _sigmoid(jnp.concatenate(gs, axis=1) + bx_ref[...])
    log_a = -RG_C * r * _softplus(-lam_ref[...])
    a = jnp.exp(log_a)
    u = jnp.sqrt(-jnp.tanh(log_a) * (a * a + 1.0)) * (g * xc)
    rowid = lax.broadcasted_iota(i32, (tr, 1), 0)
    s = 1
    while s < tr:
        if reverse:
            keep = rowid < tr - s
            shift = tr - s
        else:
            keep = rowid >= s
            shift = s
        a_sh = jnp.where(keep, pltpu.roll(a, shift, 0), 1.0)
        u_sh = jnp.where(keep, pltpu.roll(u, shift, 0), 0.0)
        u = u + a * u_sh
        a = a * a_sh
        s *= 2
    hseq = a * carry[0:1, :] + u
    o_ref[0] = hseq
    last = 0 if reverse else tr - 1
    carry[0:1, :] = hseq[last:last + 1, :]


def rglru_scan(xc, wa, wx, ba, bx, lam, nctx_blk, reverse):
    b, tt, c = xc.shape
    tr = ROW_BLOCK
    nblk = tt // tr
    if reverse:
        def blk(bi, i):
            return (bi, jnp.where(i < nctx_blk, nctx_blk - 1 - i, nblk - 1 - (i - nctx_blk)), 0)
    else:
        def blk(bi, i):
            return (bi, i, 0)
    vec = pl.BlockSpec((1, c), lambda bi, i: (0, 0))
    wspec = pl.BlockSpec((RG_HEADS, RG_HEAD_DIM, RG_HEAD_DIM), lambda bi, i: (0, 0, 0))
    return pl.pallas_call(
        functools.partial(_rglru_kernel, reverse),
        grid=(b, nblk),
        in_specs=[pl.BlockSpec((1, tr, c), blk), wspec, wspec, vec, vec, vec],
        out_specs=pl.BlockSpec((1, tr, c), blk),
        out_shape=jax.ShapeDtypeStruct((b, tt, c), f32),
        scratch_shapes=[pltpu.VMEM((SUBLANES, c), f32)],
        compiler_params=_cparams(("arbitrary", "arbitrary")),
        name="rglru_rev" if reverse else "rglru_fwd",
    )(xc, wa, wx, ba.reshape(1, c), bx.reshape(1, c), lam.reshape(1, c))


def _dft_kernel(t_len, x_ref, o_ref, c2, s2):
    mi = pl.program_id(0)
    ki = pl.program_id(1)
    bm, bk = c2.shape
    w0 = 2.0 * math.pi / t_len

    @pl.when(ki == 0)
    def _():
        k = mi * bm + lax.broadcasted_iota(i32, (bm, bk), 0)
        a = lax.broadcasted_iota(i32, (bm, bk), 1)
        ang = ((k * a) & (t_len - 1)).astype(f32) * w0
        c2[...] = jnp.cos(ang)
        s2[...] = jnp.sin(ang)
        o_ref[...] = jnp.zeros_like(o_ref)

    kcol = mi * bm + lax.broadcasted_iota(i32, (bm, 1), 0)
    ang1 = ((kcol * (ki * bk)) & (t_len - 1)).astype(f32) * w0
    c1 = jnp.cos(ang1)
    s1 = jnp.sin(ang1)
    cm = (c1 * c2[...] - s1 * s2[...]).astype(bf16)
    sm = (s1 * c2[...] + c1 * s2[...]).astype(bf16)
    w = x_ref.shape[2]
    for b in range(x_ref.shape[0]):
        xb = x_ref[b].astype(bf16)
        o_ref[b, :, 0:w] += jnp.dot(cm, xb, preferred_element_type=f32)
        o_ref[b, :, w:2 * w] += jnp.dot(sm, xb, preferred_element_type=f32)


def dft_positions(p, col_blk, width, row0, t_len):
    b = p.shape[0]
    assert t_len & (t_len - 1) == 0
    bk = min(ROW_BLOCK, t_len)
    bm = min(1024, t_len)
    r0 = row0 // bk
    return pl.pallas_call(
        functools.partial(_dft_kernel, t_len),
        grid=(t_len // bm, t_len // bk),
        in_specs=[pl.BlockSpec((b, bk, width), lambda mi, ki: (0, r0 + ki, col_blk))],
        out_specs=pl.BlockSpec((b, bm, 2 * width), lambda mi, ki: (0, mi, 0)),
        out_shape=jax.ShapeDtypeStruct((b, t_len, 2 * width), f32),
        scratch_shapes=[pltpu.VMEM((bm, bk), f32), pltpu.VMEM((bm, bk), f32)],
        compiler_params=_cparams(("arbitrary", "arbitrary")),
        name="dft_positions",
    )(p)


def _residual_and_ffn_norm(h, y, mod_ref, gf_ref, hout_ref, f_ref):
    hn = h + mod_ref[0, 2:3, :] * y
    hout_ref[0] = hn
    f_ref[0] = _rms(hn, gf_ref[...]) * (1.0 + mod_ref[0, 4:5, :]) + mod_ref[0, 3:4, :]


def _merge_even_kernel(nctx_blk, sc_ctx, sc_lat, hf_ref, hb_ref, gate_ref, z_ref, h_ref, mod_ref, cc_ref, cs_ref,
                       fw_ref, fb_ref, wa_ref, wb_ref, gf_ref, hout_ref, f_ref):
    i = pl.program_id(1)
    ya = (hf_ref[0] + hb_ref[0]) * _gelu_tanh(gate_ref[0])
    z = z_ref[0]
    w = z.shape[1] // 2
    zz = (jnp.dot(z[:, :w], cc_ref[...], precision=HIGHEST, preferred_element_type=f32)
          - jnp.dot(z[:, w:], cs_ref[...], precision=HIGHEST, preferred_element_type=f32))
    zz = zz * jnp.where(i < nctx_blk, sc_ctx, sc_lat)
    yb = jnp.dot(zz, fw_ref[...], precision=HIGHEST, preferred_element_type=f32) + fb_ref[...]
    y = (jnp.dot(ya.astype(bf16), wa_ref[...], preferred_element_type=f32)
         + jnp.dot(yb.astype(bf16), wb_ref[...], preferred_element_type=f32))
    _residual_and_ffn_norm(h_ref[0], y, mod_ref, gf_ref, hout_ref, f_ref)


def merge_even(hf, hb, p, z, h, mod, fw_bd, fb, w_out, gf, nctx_blk, t_ctx, t_lat):
    b, tt, d = h.shape
    tr = ROW_BLOCK
    jj, kk = np.meshgrid(np.arange(FN_GROUP_DIM), np.arange(FN_GROUP_DIM), indexing="ij")
    ang = 2.0 * np.pi * jj * kk / FN_GROUP_DIM
    eye = np.eye(FN_GROUPS)
    cc = jnp.asarray(np.kron(eye, np.cos(ang)), f32)
    cs = jnp.asarray(np.kron(eye, np.sin(ang)), f32)
    sc_ctx = 1.0 / math.sqrt(t_ctx * FN_GROUP_DIM)
    sc_lat = 1.0 / math.sqrt(t_lat * FN_GROUP_DIM)

    def rows(wd, cb=0):
        return pl.BlockSpec((1, tr, wd), lambda bi, i: (bi, i, cb))

    def full(shape):
        return pl.BlockSpec(shape, lambda bi, i: (0,) * len(shape))

    return pl.pallas_call(
        functools.partial(_merge_even_kernel, nctx_blk, sc_ctx, sc_lat),
        grid=(b, tt // tr),
        in_specs=[
            rows(RG_WIDTH), rows(RG_WIDTH), rows(RG_WIDTH, 1), rows(2 * FN_WIDTH), rows(d),
            pl.BlockSpec((1, SUBLANES, d), lambda bi, i: (_mod_row(bi, i, nctx_blk), 0, 0)),
            full((FN_WIDTH, FN_WIDTH)), full((FN_WIDTH, FN_WIDTH)), full((FN_WIDTH, FN_WIDTH)), full((1, FN_WIDTH)),
            full((RG_WIDTH, d)), full((FN_WIDTH, d)), full((1, d)),
        ],
        out_specs=[rows(d), rows(d)],
        out_shape=[jax.ShapeDtypeStruct((b, tt, d), f32)] * 2,
        compiler_params=_cparams(("arbitrary", "arbitrary")),
        name="merge_even",
    )(hf, hb, p, z, h, mod, cc, cs, fw_bd, fb.reshape(1, FN_WIDTH),
      w_out[:RG_WIDTH].astype(bf16), w_out[RG_WIDTH:].astype(bf16), gf.reshape(1, d))


def _merge_odd_kernel(hs0_ref, hs1_ref, o_ref, at_ref, h_ref, mod_ref, hg_ref, wm_ref, wa_ref, gf_ref, hout_ref, f_ref):
    hs = hs0_ref[0, 0] + hs1_ref[0, 0]
    parts = []
    for hd in range(ML_HEADS):
        sl = slice(hd * ML_HEAD_DIM, (hd + 1) * ML_HEAD_DIM)
        parts.append(_rms(hs[:, sl], hg_ref[:, sl]))
    yml = _sigmoid(o_ref[0]) * jnp.concatenate(parts, axis=1)
    y = (jnp.dot(yml.astype(bf16), wm_ref[...], preferred_element_type=f32)
         + jnp.dot(at_ref[0].astype(bf16), wa_ref[...], preferred_element_type=f32))
    _residual_and_ffn_norm(h_ref[0], y, mod_ref, gf_ref, hout_ref, f_ref)


def merge_odd(hs, p, attn, h, mod, head_g, w_ml, w_at, gf, nctx_blk):
    b, t, ap = attn.shape
    d = h.shape[2]
    tr = ROW_BLOCK

    def rows(wd, cb=0, off=0):
        return pl.BlockSpec((1, tr, wd), lambda bi, i: (bi, i + off, cb))

    def full(shape):
        return pl.BlockSpec(shape, lambda bi, i: (0,) * len(shape))

    return pl.pallas_call(
        _merge_odd_kernel,
        grid=(b, t // tr),
        in_specs=[
            pl.BlockSpec((1, 1, tr, ML_WIDTH), lambda bi, i: (0, bi, i + nctx_blk, 0)),
            pl.BlockSpec((1, 1, tr, ML_WIDTH), lambda bi, i: (1, bi, i + nctx_blk, 0)),
            rows(ML_WIDTH, OD_O // ML_WIDTH, nctx_blk), rows(ap), rows(d, 0, nctx_blk),
            pl.BlockSpec((1, SUBLANES, d), lambda bi, i: (bi, 0, 0)),
            full((1, ML_WIDTH)), full((ML_WIDTH, d)), full((ap, d)), full((1, d)),
        ],
        out_specs=[rows(d), rows(d)],
        out_shape=[jax.ShapeDtypeStruct((b, t, d), f32)] * 2,
        compiler_params=_cparams(("arbitrary", "arbitrary")),
        name="merge_odd",
    )(hs, hs, p, attn, h, mod, head_g.reshape(1, ML_WIDTH), w_ml, w_at, gf.reshape(1, d))


def _mlstm_kernel(qk_ref, v_ref, g_ref, gt_ref, gb_ref, gbt_ref, o_ref, ct, nst, mst):
    d = pl.program_id(0)
    c = pl.program_id(2)
    L = ML_CHUNK

    @pl.when(c == 0)
    def _():
        ct[...] = jnp.zeros_like(ct)
        nst[...] = jnp.zeros_like(nst)
        mst[...] = jnp.zeros_like(mst)

    ri = lax.broadcasted_iota(i32, (L, L), 0)
    ci = lax.broadcasted_iota(i32, (L, L), 1)
    sgn = 1 - 2 * d
    tri = (ci - ri) * sgn <= 0
    trit = (ri - ci) * sgn <= 0
    gcol = g_ref[0] + gb_ref[...]
    grow = gt_ref[0] + gbt_ref[...]
    lane = lax.broadcasted_iota(i32, (1, LANES), 1)
    for hd in range(ML_HEADS):
        sl = slice(hd * ML_HEAD_DIM, (hd + 1) * ML_HEAD_DIM)
        q = qk_ref[0, :, sl] * (ML_HEAD_DIM ** -0.5)
        k = qk_ref[0, :, ML_WIDTH + hd * ML_HEAD_DIM:ML_WIDTH + (hd + 1) * ML_HEAD_DIM]
        v = v_ref[0, :, sl]
        li = d * 8 + hd
        ig_col = jnp.sum(jnp.where(lane == li, gcol, 0.0), axis=1, keepdims=True)
        fg_col = jnp.sum(jnp.where(lane == li + 4, gcol, 0.0), axis=1, keepdims=True)
        rsel = lax.broadcasted_iota(i32, (ML_N_GATES, 1), 0)
        ig_row = jnp.sum(jnp.where(rsel == li, grow, 0.0), axis=0, keepdims=True)
        fg_row = jnp.sum(jnp.where(rsel == li + 4, grow, 0.0), axis=0, keepdims=True)
        lf_col = -_softplus(-fg_col)
        lf_row = -_softplus(-fg_row)
        b_col = jnp.sum(jnp.where(tri, lf_row, 0.0), axis=1, keepdims=True)
        b_row = jnp.sum(jnp.where(trit, lf_col, 0.0), axis=0, keepdims=True)
        m_prev = mst[hd, 0:1, 0:1]
        n_prev = nst[hd, 0:1, :]
        ct_prev = ct[hd]
        dm = jnp.where(tri, b_col - b_row + ig_row, -jnp.inf)
        inter = b_col + m_prev
        m_row = jnp.maximum(jnp.max(dm, axis=1, keepdims=True), inter)
        qb, kb, vb = q.astype(bf16), k.astype(bf16), v.astype(bf16)
        s = lax.dot_general(qb, kb, (((1,), (1,)), ((), ())), preferred_element_type=f32) * jnp.exp(dm - m_row)
        w_inter = jnp.exp(inter - m_row)
        num = (jnp.dot(s.astype(bf16), vb, preferred_element_type=f32)
               + w_inter * jnp.dot(qb, ct_prev.astype(bf16), preferred_element_type=f32))
        den = jnp.sum(s, axis=1, keepdims=True) + w_inter * jnp.sum(q * n_prev, axis=1, keepdims=True)
        o_ref[0, 0, :, sl] = num / jnp.maximum(jnp.abs(den), jnp.exp(-m_row))
        b_l = jnp.sum(lf_col, axis=0, keepdims=True)
        log_wk = b_l - b_col + ig_col
        m_new = jnp.maximum(b_l + m_prev, jnp.max(log_wk, axis=0, keepdims=True))
        kw = jnp.exp(log_wk - m_new) * k
        decay = jnp.exp(b_l + m_prev - m_new)
        ct[hd] = decay * ct_prev + lax.dot_general(kw.astype(bf16), vb, (((0,), (0,)), ((), ())),
                                                   preferred_element_type=f32)
        nst[hd, 0:1, :] = decay * n_prev + jnp.sum(kw, axis=0, keepdims=True)
        mst[hd] = jnp.broadcast_to(m_new, (SUBLANES, LANES))


def mlstm(qkc, p, gt, gate_b, nctx_chunks):
    b, tt, _ = qkc.shape
    L = ML_CHUNK
    nch = tt // L

    def chunk(d, c):
        return jnp.where(d == 0, c, jnp.where(c < nctx_chunks, nctx_chunks - 1 - c, nch - 1 - (c - nctx_chunks)))

    gb = jnp.zeros((1, LANES), f32).at[0, :ML_N_GATES].set(gate_b)
    return pl.pallas_call(
        _mlstm_kernel,
        grid=(2, b, nch),
        in_specs=[
            pl.BlockSpec((1, L, 2 * ML_WIDTH), lambda d, bi, c: (bi, chunk(d, c), 0)),
            pl.BlockSpec((1, L, ML_WIDTH), lambda d, bi, c: (bi, chunk(d, c), OD_V // ML_WIDTH)),
            pl.BlockSpec((1, L, LANES), lambda d, bi, c: (bi, chunk(d, c), OD_G // LANES)),
            pl.BlockSpec((1, ML_N_GATES, L), lambda d, bi, c: (bi, 0, chunk(d, c))),
            pl.BlockSpec((1, LANES), lambda d, bi, c: (0, 0)),
            pl.BlockSpec((ML_N_GATES, 1), lambda d, bi, c: (0, 0)),
        ],
        out_specs=pl.BlockSpec((1, 1, L, ML_WIDTH), lambda d, bi, c: (d, bi, chunk(d, c), 0)),
        out_shape=jax.ShapeDtypeStruct((2, b, tt, ML_WIDTH), f32),
        scratch_shapes=[
            pltpu.VMEM((ML_HEADS, ML_HEAD_DIM, ML_HEAD_DIM), f32),
            pltpu.VMEM((ML_HEADS, SUBLANES, LANES), f32),
            pltpu.VMEM((ML_HEADS, SUBLANES, LANES), f32),
        ],
        compiler_params=_cparams(("arbitrary", "arbitrary", "arbitrary")),
        name="mlstm",
    )(qkc, p, p, gt, gb, gate_b.reshape(ML_N_GATES, 1))


def _mla_prep_kernel(nctx_blk, p_ref, qg_ref, kg_ref, wq_ref, wqs_ref, wk_ref, wv_ref, em_ref, es_ref,
                     q_ref, k_ref, v_ref):
    i = pl.program_id(1)
    tr = p_ref.shape[1]
    cq = _rms(p_ref[0, :, OD_CQ:OD_CKV], qg_ref[...]).astype(bf16)
    ckv = _rms(p_ref[0, :, OD_CKV:OD_KR], kg_ref[...]).astype(bf16)
    kr = p_ref[0, :, OD_KR:OD_N]
    lane = lax.broadcasted_iota(i32, (1, MLA_HEAD_PAD), 1)
    half = MLA_ROPE // 2
    pair = jnp.where(lane < MLA_NOPE + half, lane - MLA_NOPE, lane - MLA_NOPE - half)
    is_rope = jnp.logical_and(lane >= MLA_NOPE, lane < MLA_NOPE + MLA_ROPE)
    use_row = pair < half // 2
    fidx = jnp.clip(jnp.where(use_row, pair, pair - half // 2), 0, half // 2 - 1).astype(f32)
    inv = jnp.exp(fidx * (-math.log(ROPE_THETA) / (half // 2)))
    t = (i - nctx_blk) * tr + lax.broadcasted_iota(i32, (tr, 1), 0)
    pos = jnp.where(use_row, (t >> GRID_SHIFT).astype(f32), (t & (GRID_W - 1)).astype(f32))
    ang = pos * inv
    rot = jnp.logical_and(is_rope, i >= nctx_blk)
    cos1 = jnp.where(rot, jnp.cos(ang), 1.0)
    sin1 = jnp.where(rot, jnp.sin(ang), 0.0)
    cos_t = jnp.concatenate([cos1] * MLA_HEADS, axis=1)
    sin_t = jnp.concatenate([sin1] * MLA_HEADS, axis=1)
    scale = (MLA_NOPE + MLA_ROPE) ** -0.5
    q = (jnp.dot(cq, wq_ref[...], preferred_element_type=f32) * cos_t
         + jnp.dot(cq, wqs_ref[...], preferred_element_type=f32) * sin_t)
    q_ref[0] = (q * scale).astype(bf16)
    k = (jnp.dot(ckv, wk_ref[...], preferred_element_type=f32)
         + jnp.dot(kr, em_ref[...], precision=HIGHEST, preferred_element_type=f32) * cos_t
         + jnp.dot(kr, es_ref[...], precision=HIGHEST, preferred_element_type=f32) * sin_t)
    k_ref[0] = k.astype(bf16)
    v_ref[0] = jnp.dot(ckv, wv_ref[...], preferred_element_type=f32).astype(bf16)


def _mla_weights(w_uq, w_ukv):
    hp = MLA_HEAD_PAD
    half = MLA_ROPE // 2
    wq = w_uq.reshape(MLA_Q_LORA, MLA_HEADS, MLA_NOPE + MLA_ROPE)
    nope, x1, x2 = wq[..., :MLA_NOPE], wq[..., MLA_NOPE::2], wq[..., MLA_NOPE + 1::2]
    zq = jnp.zeros((MLA_Q_LORA, MLA_HEADS, hp - MLA_NOPE - MLA_ROPE), f32)
    wq_main = jnp.concatenate([nope, x1, x2, zq], axis=-1).reshape(MLA_Q_LORA, MLA_HEADS * hp)
    wq_swap = jnp.concatenate([jnp.zeros_like(nope), -x2, x1, zq], axis=-1).reshape(MLA_Q_LORA, MLA_HEADS * hp)
    wkv = w_ukv.reshape(MLA_KV_LORA, MLA_HEADS, MLA_NOPE + MLA_V)
    zk = jnp.zeros((MLA_KV_LORA, MLA_HEADS, hp - MLA_NOPE), f32)
    wk = jnp.concatenate([wkv[..., :MLA_NOPE], zk], axis=-1).reshape(MLA_KV_LORA, MLA_HEADS * hp)
    wv = jnp.concatenate([wkv[..., MLA_NOPE:], jnp.zeros((MLA_KV_LORA, MLA_HEADS, hp - MLA_V), f32)],
                         axis=-1).reshape(MLA_KV_LORA, MLA_HEADS * hp)
    em = np.zeros((LANES, MLA_HEADS, hp), np.float32)
    es = np.zeros((LANES, MLA_HEADS, hp), np.float32)
    for j in range(half):
        em[2 * j, :, MLA_NOPE + j] = 1.0
        em[2 * j + 1, :, MLA_NOPE + half + j] = 1.0
        es[2 * j + 1, :, MLA_NOPE + j] = -1.0
        es[2 * j, :, MLA_NOPE + half + j] = 1.0
    em = jnp.asarray(em.reshape(LANES, MLA_HEADS * hp))
    es = jnp.asarray(es.reshape(LANES, MLA_HEADS * hp))
    return wq_main.astype(bf16), wq_swap.astype(bf16), wk.astype(bf16), wv.astype(bf16), em, es


def mla_prep(p, q_norm_g, kv_norm_g, w_uq, w_ukv, nctx_blk):
    b, tt, n = p.shape
    tr = ROW_BLOCK
    hw = MLA_HEADS * MLA_HEAD_PAD
    wq, wqs, wk, wv, em, es = _mla_weights(w_uq, w_ukv)

    def full(shape):
        return pl.BlockSpec(shape, lambda bi, i: (0,) * len(shape))

    out = pl.BlockSpec((1, tr, hw), lambda bi, i: (bi, i, 0))
    return pl.pallas_call(
        functools.partial(_mla_prep_kernel, nctx_blk),
        grid=(b, tt // tr),
        in_specs=[pl.BlockSpec((1, tr, n), lambda bi, i: (bi, i, 0)),
                  full((1, MLA_Q_LORA)), full((1, MLA_KV_LORA)),
                  full((MLA_Q_LORA, hw)), full((MLA_Q_LORA, hw)), full((MLA_KV_LORA, hw)), full((MLA_KV_LORA, hw)),
                  full((LANES, hw)), full((LANES, hw))],
        out_specs=[out, out, out],
        out_shape=[jax.ShapeDtypeStruct((b, tt, hw), bf16)] * 3,
        compiler_params=_cparams(("arbitrary", "arbitrary")),
        name="mla_prep",
    )(p, q_norm_g.reshape(1, MLA_Q_LORA), kv_norm_g.reshape(1, MLA_KV_LORA), wq, wqs, wk, wv, em, es)


def _mla_attn_kernel(q_ref, k_ref, v_ref, o_ref):
    s = lax.dot_general(q_ref[0], k_ref[0], (((1,), (1,)), ((), ())), preferred_element_type=f32)
    m = jnp.max(s, axis=-1, keepdims=True)
    e = jnp.exp(s - m)
    l = jnp.sum(e, axis=-1, keepdims=True)
    o_ref[0] = jnp.dot(e.astype(bf16), v_ref[0], preferred_element_type=f32) / l


def mla_attention(q, k, v, nctx_blk):
    b, tt, hw = q.shape
    tq = ROW_BLOCK
    nq = tt // tq - nctx_blk
    hp = MLA_HEAD_PAD
    kv = pl.BlockSpec((1, tt, hp), lambda bi, h, qi: (bi, 0, h))
    return pl.pallas_call(
        _mla_attn_kernel,
        grid=(b, MLA_HEADS, nq),
        in_specs=[pl.BlockSpec((1, tq, hp), lambda bi, h, qi: (bi, qi + nctx_blk, h)), kv, kv],
        out_specs=pl.BlockSpec((1, tq, hp), lambda bi, h, qi: (bi, qi, h)),
        out_shape=jax.ShapeDtypeStruct((b, nq * tq, hw), f32),
        compiler_params=_cparams(("arbitrary", "arbitrary", "arbitrary")),
        name="mla_attention",
    )(q, k, v)


def _topk_rows(s, k, payload=None):
    n = s.shape[0]
    rid = lax.broadcasted_iota(i32, s.shape, 0)
    vals, outs = [], []
    for _ in range(k):
        m = jnp.max(s, axis=0, keepdims=True)
        am = jnp.min(jnp.where(s == m, rid, n), axis=0, keepdims=True)
        hit = rid == am
        vals.append(m)
        outs.append(am if payload is None else jnp.sum(jnp.where(hit, payload, 0), axis=0, keepdims=True))
        s = jnp.where(hit, -jnp.inf, s)
    return jnp.concatenate(vals, axis=0), jnp.concatenate(outs, axis=0)


def _pruned_candidates(vals, idxs):
    k = PEER_TOPK
    hk = k // 2
    bcol = lax.broadcasted_iota(i32, (hk, 1), 0)
    cand = [vals[0][0:1, :] + vals[1]]
    cidx = [idxs[0][0:1, :] * PEER_N_KEYS + idxs[1]]
    for a in range(1, hk):
        ok = (a + 1) * (bcol + 1) <= k
        cand.append(jnp.where(ok, vals[0][a:a + 1, :] + vals[1][0:hk, :], -jnp.inf))
        cidx.append(idxs[0][a:a + 1, :] * PEER_N_KEYS + idxs[1][0:hk, :])
    cand.append(vals[0][hk:k, :] + vals[1][0:1, :])
    cidx.append(idxs[0][hk:k, :] * PEER_N_KEYS + idxs[1][0:1, :])
    return jnp.concatenate(cand, axis=0), jnp.concatenate(cidx, axis=0)


def _peer_route_kernel(f_ref, wq_ref, keys_ref, code_ref, g_ref):
    q = jnp.dot(f_ref[...].astype(bf16), wq_ref[...], preferred_element_type=f32)
    hk = PEER_KEY_DIM // 2
    for h in range(PEER_HEADS):
        vals, idxs = [], []
        for p in range(2):
            c0 = (h * 2 + p) * hk
            qhp = q[:, c0:c0 + hk].astype(bf16)
            st = lax.dot_general(keys_ref[p, h], qhp, (((1,), (1,)), ((), ())), preferred_element_type=f32)
            v, ix = _topk_rows(st, PEER_TOPK)
            vals.append(v)
            idxs.append(ix)
        cand, cidx = _pruned_candidates(vals, idxs)
        top_s, e = _topk_rows(cand, PEER_TOPK, payload=cidx)
        ex = jnp.exp(top_s - top_s[0:1, :])
        rows = slice(h * PEER_TOPK, (h + 1) * PEER_TOPK)
        g_ref[rows, :] = ex / jnp.sum(ex, axis=0, keepdims=True)
        code_ref[rows, :] = (e >> 1) * SUBLANES + (1 - (e & 1))


def peer_route(f2, wq, keys):
    r, d = f2.shape
    tb = PEER_TB
    out = pl.BlockSpec((PEER_PICKS, tb), lambda i: (0, i))
    return pl.pallas_call(
        _peer_route_kernel,
        grid=(r // tb,),
        in_specs=[
            pl.BlockSpec((tb, d), lambda i: (i, 0)),
            pl.BlockSpec(wq.shape, lambda i: (0, 0)),
            pl.BlockSpec(keys.shape, lambda i: (0, 0, 0, 0)),
        ],
        out_specs=[out, out],
        out_shape=[jax.ShapeDtypeStruct((PEER_PICKS, r), i32), jax.ShapeDtypeStruct((PEER_PICKS, r), f32)],
        compiler_params=_cparams(("arbitrary",)),
        name="peer_route",
    )(f2, wq, keys)


def pack_expert_table(t):
    e, d = t.shape
    bits = lax.bitcast_convert_type(t.astype(bf16), jnp.uint16).astype(jnp.uint32)
    bits = bits.reshape(e // 2, 2, SUBLANES, d // SUBLANES)
    word = bits[:, 0] | (bits[:, 1] << 16)
    return lax.bitcast_convert_type(word, i32).reshape(e // 2 * SUBLANES, d // SUBLANES)


def _expert_row(tab_ref, code):
    word = tab_ref[pl.ds(pl.multiple_of(code & ROW_MASK, SUBLANES), SUBLANES), :]
    cv = jnp.full((SUBLANES, LANES), code, i32)
    return pltpu.bitcast((word << ((cv & 1) << 4)) & HI_MASK, f32), cv


def _sublane_sums(ps, row):
    lo4 = row < 4
    t = []
    for s in range(4):
        a, b = ps[s], ps[s + 4]
        t.append(jnp.where(lo4, a, b) + pltpu.roll(jnp.where(lo4, b, a), 4, 0))
    m2 = (row & 2) == 0
    u = []
    for s in range(2):
        a, b = t[s], t[s + 2]
        u.append(jnp.where(m2, a, b) + jnp.where(m2, pltpu.roll(a, 6, 0), pltpu.roll(b, 2, 0)))
    m1 = (row & 1) == 0
    a, b = u[0], u[1]
    return jnp.where(m1, a, b) + jnp.where(m1, pltpu.roll(a, 7, 0), pltpu.roll(b, 1, 0))


def _load_table_once(tab_hbm, tab_vmem, sem):
    @pl.when(pl.program_id(0) == 0)
    def _():
        cp = pltpu.make_async_copy(tab_hbm, tab_vmem, sem)
        cp.start()
        cp.wait()


def _fetch_block(src_hbm, dst_smem, sem, i):
    n = dst_smem.shape[0]
    cp = pltpu.make_async_copy(src_hbm.at[pl.ds(i * n, n)], dst_smem, sem)
    cp.start()
    return cp


def _peer_act_kernel(code_hbm, code_ref, x_ref, g_ref, tab_hbm, cw_ref, tab_vmem, code_s, rbuf, sem_tab, sem_a):
    i = pl.program_id(0)
    tb = x_ref.shape[0]
    _load_table_once(tab_hbm, tab_vmem, sem_tab)
    _fetch_block(code_hbm, code_s, sem_a, i).wait()
    row = lax.broadcasted_iota(i32, (SUBLANES, LANES), 0)
    ng = PEER_PICKS // SUBLANES

    def tok(t, c):
        x = x_ref[t]

        def grp(jg, c2):
            base = t * PEER_PICKS + jg * SUBLANES
            ps = [_expert_row(tab_vmem, code_s[base + s])[0] * x for s in range(SUBLANES)]
            rbuf[t * ng + jg] = _sublane_sums(ps, row)
            return c2

        return lax.fori_loop(0, ng, grp, c, unroll=PEER_GROUP_UNROLL)

    lax.fori_loop(0, tb, tok, 0)
    act = jnp.sum(rbuf[...].reshape(tb, PEER_PICKS, LANES), axis=-1)
    w = (g_ref[...] * _gelu_tanh(act)).astype(bf16).astype(f32)
    cw_ref[...] = pltpu.bitcast(w, i32) | code_ref[...]


def peer_act(code_flat, code, x3, g, tab):
    r = x3.shape[0]
    tb = PEER_TB
    anyspec = pl.BlockSpec(memory_space=pl.ANY)
    rows = pl.BlockSpec((tb, PEER_PICKS), lambda i: (i, 0))
    return pl.pallas_call(
        _peer_act_kernel,
        grid=(r // tb,),
        in_specs=[anyspec, rows, pl.BlockSpec((tb, SUBLANES, LANES), lambda i: (i, 0, 0)), rows, anyspec],
        out_specs=rows,
        out_shape=jax.ShapeDtypeStruct((r, PEER_PICKS), i32),
        scratch_shapes=[
            pltpu.VMEM(tab.shape, i32),
            pltpu.SMEM((tb * PEER_PICKS,), i32),
            pltpu.VMEM((tb * PEER_PICKS // SUBLANES, SUBLANES, LANES), f32),
            pltpu.SemaphoreType.DMA, pltpu.SemaphoreType.DMA,
        ],
        compiler_params=_cparams(("arbitrary",), VMEM_LIMIT_TABLE),
        name="peer_act",
    )(code_flat, code, x3, g, tab)


def _peer_out_kernel(cw_hbm, tab_hbm, o_ref, tab_vmem, cw_s, sem_tab, sem_a):
    i = pl.program_id(0)
    tb = o_ref.shape[0]
    _load_table_once(tab_hbm, tab_vmem, sem_tab)
    _fetch_block(cw_hbm, cw_s, sem_a, i).wait()
    nacc = 4

    def tok(t, c):
        def grp(jg, accs):
            base = t * PEER_PICKS + jg * SUBLANES
            accs = list(accs)
            for s in range(SUBLANES):
                tile, cv = _expert_row(tab_vmem, cw_s[base + s])
                accs[s % nacc] = accs[s % nacc] + pltpu.bitcast(cv & HI_MASK, f32) * tile
            return tuple(accs)

        z = jnp.zeros((SUBLANES, LANES), f32)
        accs = lax.fori_loop(0, PEER_PICKS // SUBLANES, grp, (z,) * nacc, unroll=PEER_GROUP_UNROLL)
        o_ref[t] = (accs[0] + accs[1]) + (accs[2] + accs[3])
        return c

    lax.fori_loop(0, tb, tok, 0)


def peer_out(cw_flat, tab):
    r = cw_flat.shape[0] // PEER_PICKS
    tb = PEER_TB
    anyspec = pl.BlockSpec(memory_space=pl.ANY)
    return pl.pallas_call(
        _peer_out_kernel,
        grid=(r // tb,),
        in_specs=[anyspec, anyspec],
        out_specs=pl.BlockSpec((tb, SUBLANES, LANES), lambda i: (i, 0, 0)),
        out_shape=jax.ShapeDtypeStruct((r, SUBLANES, LANES), f32),
        scratch_shapes=[
            pltpu.VMEM(tab.shape, i32),
            pltpu.SMEM((tb * PEER_PICKS,), i32),
            pltpu.SemaphoreType.DMA, pltpu.SemaphoreType.DMA,
        ],
        compiler_params=_cparams(("arbitrary",), VMEM_LIMIT_TABLE),
        name="peer_out",
    )(cw_flat, tab)


def peer_ffn(f, wq, keys, u, v):
    b, r, d = f.shape
    f2 = f.reshape(b * r, d)
    codet, gt = peer_route(f2, wq.astype(bf16), keys.astype(bf16))
    code = codet.T
    cw = peer_act(code.reshape(-1), code, f2.reshape(b * r, SUBLANES, d // SUBLANES), gt.T, pack_expert_table(u))
    o = peer_out(cw.reshape(-1), pack_expert_table(v))
    return o.reshape(b, r, d)


def _final_kernel(h_ref, peer_ref, mod_ref, g_ref, o_ref):
    h = h_ref[0] + mod_ref[0, 5:6, :] * peer_ref[0]
    o_ref[0] = _rms(h, g_ref[...])


def final_norm(h, peer, mod, g):
    b, t, d = h.shape
    tr = ROW_BLOCK
    row = pl.BlockSpec((1, tr, d), lambda bi, i: (bi, i, 0))
    return pl.pallas_call(
        _final_kernel,
        grid=(b, t // tr),
        in_specs=[row, row, pl.BlockSpec((1, SUBLANES, d), lambda bi, i: (bi, 0, 0)),
                  pl.BlockSpec((1, d), lambda bi, i: (0, 0))],
        out_specs=row,
        out_shape=jax.ShapeDtypeStruct((b, t, d), f32),
        compiler_params=_cparams(("arbitrary", "arbitrary")),
        name="final_norm",
    )(h, peer, mod, g.reshape(1, d))


def _block_diag(w):
    g, a, c = w.shape
    out = jnp.zeros((g * a, g * c), w.dtype)
    for k in range(g):
        out = out.at[k * a:(k + 1) * a, k * c:(k + 1) * c].set(w[k])
    return out


def kernel(x, c, ctx, c_ctx, mod_w, mod_b, norm_mix_g, norm_ffn_g, ev_w_in, ev_w_out, rg_conv_w, rg_conv_b, rg_wa, rg_ba, rg_wx, rg_bx, rg_lam, fn_w, fn_b, od_w_in, od_w_out, ml_conv_w, ml_conv_b, ml_gate_b, ml_head_g, mla_q_norm_g, mla_w_uq, mla_kv_norm_g, mla_w_ukv, peer_w_q, peer_keys, peer_u, peer_v, final_g):
    b, t, d = x.shape
    tc = ctx.shape[1]
    assert b == 2 and tc % ROW_BLOCK == 0 and t % ROW_BLOCK == 0 and mod_w.shape[0] == 2
    nctx_blk = tc // ROW_BLOCK

    cvec = jnp.zeros((SUBLANES, d), f32).at[0:b].set(c).at[b].set(c_ctx)
    mods = modulation(cvec, mod_w, mod_b)
    mods = mods.reshape(2, SUBLANES, 6, d)[:, :3]
    mods = jnp.pad(mods, ((0, 0), (0, 0), (0, 2), (0, 0)))

    h0 = jnp.concatenate([ctx, x], axis=1)

    p = inproj(h0, mods[0], norm_mix_g[0], ev_w_in[0].astype(bf16), nctx_blk)
    xc = dwconv(p, RG_WIDTH, rg_conv_w[0], rg_conv_b[0], nctx_blk, apply_silu=False)
    hf = rglru_scan(xc, rg_wa[0, 0], rg_wx[0, 0], rg_ba[0, 0], rg_bx[0, 0], rg_lam[0, 0], nctx_blk, reverse=False)
    hb = rglru_scan(xc, rg_wa[0, 1], rg_wx[0, 1], rg_ba[0, 1], rg_bx[0, 1], rg_lam[0, 1], nctx_blk, reverse=True)
    fcol = 2 * RG_WIDTH // FN_WIDTH
    z = jnp.concatenate([dft_positions(p, fcol, FN_WIDTH, 0, tc), dft_positions(p, fcol, FN_WIDTH, tc, t)], axis=1)
    h1, f1 = merge_even(hf, hb, p, z, h0, mods[0], _block_diag(fn_w[0]), fn_b[0], ev_w_out[0], norm_ffn_g[0],
                        nctx_blk, tc, t)
    peer1 = peer_ffn(f1, peer_w_q[0], peer_keys[0], peer_u[0], peer_v[0])

    w = od_w_in[0]
    zpad = lambda n: jnp.zeros((d, n), f32)
    w_od = jnp.concatenate([w[:, :OD_G + ML_N_GATES], zpad(OD_CQ - OD_G - ML_N_GATES),
                            w[:, 2064:2064 + MLA_Q_LORA + MLA_KV_LORA + MLA_ROPE],
                            zpad(OD_N - OD_KR - MLA_ROPE)], axis=1).astype(bf16)
    h2, p2 = inproj(h1, mods[1], norm_mix_g[1], w_od, nctx_blk, peer=peer1, mod_prev=mods[0])
    qkc = dwconv(p2, 2 * ML_WIDTH, ml_conv_w[0], ml_conv_b[0], nctx_blk, apply_silu=True)
    gt = jnp.swapaxes(p2[:, :, OD_G:OD_G + ML_N_GATES], 1, 2)
    hs = mlstm(qkc, p2, gt, ml_gate_b[0], tc // ML_CHUNK)
    q, k, v = mla_prep(p2, mla_q_norm_g[0], mla_kv_norm_g[0], mla_w_uq[0], mla_w_ukv[0], nctx_blk)
    attn = mla_attention(q, k, v, nctx_blk)
    wo = od_w_out[0]
    w_at = jnp.concatenate([wo[ML_WIDTH:].reshape(MLA_HEADS, MLA_V, d),
                            jnp.zeros((MLA_HEADS, MLA_HEAD_PAD - MLA_V, d), f32)], axis=1)
    w_at = w_at.reshape(MLA_HEADS * MLA_HEAD_PAD, d).astype(bf16)
    h3, f3 = merge_odd(hs, p2, attn, h2, mods[1, :b], ml_head_g[0], wo[:ML_WIDTH].astype(bf16), w_at,
                       norm_ffn_g[1], nctx_blk)
    peer3 = peer_ffn(f3, peer_w_q[1], peer_keys[1], peer_u[1], peer_v[1])
    return final_norm(h3, peer3, mods[1, :b], final_g)
```

```python
import functools
import math

import numpy as np
import jax
import jax.numpy as jnp
from jax import lax
from jax.experimental import pallas as pl
from jax.experimental.pallas import tpu as pltpu

f32 = jnp.float32
bf16 = jnp.bfloat16
i32 = jnp.int32
HIGHEST = lax.Precision.HIGHEST

SUBLANES = 8
LANES = 128
VMEM_LIMIT = 48 * 1024 * 1024
VMEM_LIMIT_TABLE = 56 * 1024 * 1024

EPS = 1e-6
GRID_W = 64
GRID_SHIFT = 6
ROW_BLOCK = 256
CONV_W = 4
RG_HEADS, RG_HEAD_DIM = 6, 128
RG_WIDTH = RG_HEADS * RG_HEAD_DIM
RG_C = 8.0
FN_GROUPS, FN_GROUP_DIM = 4, 64
FN_WIDTH = FN_GROUPS * FN_GROUP_DIM
ML_HEADS, ML_HEAD_DIM = 4, 128
ML_WIDTH = ML_HEADS * ML_HEAD_DIM
ML_CHUNK = 128
ML_N_GATES = 2 * 2 * ML_HEADS
MLA_HEADS, MLA_NOPE, MLA_ROPE, MLA_V = 8, 64, 32, 64
MLA_Q_LORA, MLA_KV_LORA = 384, 256
MLA_HEAD_PAD = 128
ROPE_THETA = 10000.0
PEER_HEADS, PEER_KEY_DIM, PEER_N_KEYS, PEER_TOPK = 8, 256, 128, 16
PEER_PICKS = PEER_HEADS * PEER_TOPK
PEER_TB = 128
PEER_TILE_DEPTH = 8
ROW_MASK = 0xFFF8
LOW_HALF_BIT = 20

OD_V, OD_O, OD_G, OD_CQ, OD_CKV, OD_KR, OD_N = 1024, 1536, 2048, 2176, 2560, 2816, 2944


def _cparams(sem, vmem=VMEM_LIMIT):
    return pltpu.CompilerParams(dimension_semantics=sem, vmem_limit_bytes=vmem)


def _gelu_tanh(x):
    return 0.5 * x * (1.0 + jnp.tanh(0.7978845608028654 * (x + 0.044715 * x * x * x)))


def _sigmoid(x):
    return 1.0 / (1.0 + jnp.exp(-x))


def _softplus(x):
    return jnp.maximum(x, 0.0) + jnp.log1p(jnp.exp(-jnp.abs(x)))


def _rms(x, g):
    return x * lax.rsqrt(jnp.mean(x * x, axis=-1, keepdims=True) + EPS) * g


def _mod_row(b, i, nctx_blk):
    return jnp.where(i < nctx_blk, 2, b)


def _mod_kernel(c_ref, w_ref, b_ref, o_ref):
    c = c_ref[...]
    s = c * _sigmoid(c)
    o_ref[0] = jnp.dot(s, w_ref[0], precision=HIGHEST, preferred_element_type=f32) + b_ref[0]


def modulation(cvec, mod_w, mod_b):
    depth, d, n = mod_w.shape
    tn = 1536
    return pl.pallas_call(
        _mod_kernel,
        grid=(depth, n // tn),
        in_specs=[
            pl.BlockSpec((SUBLANES, d), lambda l, j: (0, 0)),
            pl.BlockSpec((1, d, tn), lambda l, j: (l, 0, j)),
            pl.BlockSpec((1, 1, tn), lambda l, j: (l, 0, j)),
        ],
        out_specs=pl.BlockSpec((1, SUBLANES, tn), lambda l, j: (l, 0, j)),
        out_shape=jax.ShapeDtypeStruct((depth, SUBLANES, n), f32),
        compiler_params=_cparams(("arbitrary", "arbitrary")),
        name="modulation",
    )(cvec, mod_w, mod_b.reshape(depth, 1, n))


def _inproj_kernel(has_res, *refs):
    if has_res:
        h_ref, peer_ref, modp_ref, mod_ref, g_ref, w_ref, hout_ref, p_ref = refs
        h = h_ref[0] + modp_ref[0, 5:6, :] * peer_ref[0]
        hout_ref[0] = h
    else:
        h_ref, mod_ref, g_ref, w_ref, p_ref = refs
        h = h_ref[0]
    u = _rms(h, g_ref[...]) * (1.0 + mod_ref[0, 1:2, :]) + mod_ref[0, 0:1, :]
    p_ref[0] = jnp.dot(u.astype(bf16), w_ref[...], preferred_element_type=f32)


def inproj(h, mod, g, w, nctx_blk, peer=None, mod_prev=None):
    b, tt, d = h.shape
    n = w.shape[1]
    tr = ROW_BLOCK
    has_res = peer is not None
    row = pl.BlockSpec((1, tr, d), lambda bi, i: (bi, i, 0))
    modspec = pl.BlockSpec((1, SUBLANES, d), lambda bi, i: (_mod_row(bi, i, nctx_blk), 0, 0))
    in_specs = [row] + ([row, modspec] if has_res else []) + [
        modspec,
        pl.BlockSpec((1, d), lambda bi, i: (0, 0)),
        pl.BlockSpec((d, n), lambda bi, i: (0, 0)),
    ]
    pspec = pl.BlockSpec((1, tr, n), lambda bi, i: (bi, i, 0))
    pshape = jax.ShapeDtypeStruct((b, tt, n), f32)
    args = [h] + ([peer, mod_prev] if has_res else []) + [mod, g.reshape(1, d), w]
    return pl.pallas_call(
        functools.partial(_inproj_kernel, has_res),
        grid=(b, tt // tr),
        in_specs=in_specs,
        out_specs=[row, pspec] if has_res else pspec,
        out_shape=[jax.ShapeDtypeStruct((b, tt, d), f32), pshape] if has_res else pshape,
        compiler_params=_cparams(("arbitrary", "arbitrary")),
        name="inproj",
    )(*args)


def _conv_kernel(apply_silu, nctx_blk, nblk, prev_ref, cur_ref, next_ref, w_ref, b_ref, o_ref, ext):
    i = pl.program_id(1)
    tr = cur_ref.shape[1]
    has_prev = jnp.logical_and(i != 0, i != nctx_blk)
    has_next = jnp.logical_and(i != nctx_blk - 1, i != nblk - 1)
    ext[0:SUBLANES] = jnp.where(has_prev, prev_ref[0], 0.0)
    ext[SUBLANES:SUBLANES + tr] = cur_ref[0]
    ext[SUBLANES + tr:2 * SUBLANES + tr] = jnp.where(has_next, next_ref[0], 0.0)
    lo = CONV_W // 2
    y = b_ref[...] + w_ref[0:1, :] * ext[pl.ds(SUBLANES - lo, tr), :]
    for j in range(1, CONV_W):
        y = y + w_ref[j:j + 1, :] * ext[pl.ds(SUBLANES - lo + j, tr), :]
    if apply_silu:
        y = y * _sigmoid(y)
    o_ref[0] = y


def dwconv(p, c, w, bias, nctx_blk, apply_silu):
    b, tt, _ = p.shape
    tr = ROW_BLOCK
    nblk = tt // tr
    hb = tr // SUBLANES
    return pl.pallas_call(
        functools.partial(_conv_kernel, apply_silu, nctx_blk, nblk),
        grid=(b, nblk),
        in_specs=[
            pl.BlockSpec((1, SUBLANES, c), lambda bi, i: (bi, jnp.maximum(i * hb - 1, 0), 0)),
            pl.BlockSpec((1, tr, c), lambda bi, i: (bi, i, 0)),
            pl.BlockSpec((1, SUBLANES, c), lambda bi, i: (bi, jnp.minimum((i + 1) * hb, nblk * hb - 1), 0)),
            pl.BlockSpec((CONV_W, c), lambda bi, i: (0, 0)),
            pl.BlockSpec((1, c), lambda bi, i: (0, 0)),
        ],
        out_specs=pl.BlockSpec((1, tr, c), lambda bi, i: (bi, i, 0)),
        out_shape=jax.ShapeDtypeStruct((b, tt, c), f32),
        scratch_shapes=[pltpu.VMEM((tr + 2 * SUBLANES, c), f32)],
        compiler_params=_cparams(("arbitrary", "arbitrary")),
        name="dwconv",
    )(p, p, p, w, bias.reshape(1, c))


def _rglru_kernel(reverse, xc_ref, wa_ref, wx_ref, ba_ref, bx_ref, lam_ref, o_ref, carry):
    i = pl.program_id(1)

    @pl.when(i == 0)
    def _():
        carry[...] = jnp.zeros_like(carry)

    xc = xc_ref[0]
    tr = xc.shape[0]
    rs, gs = [], []
    for h in range(RG_HEADS):
        xh = xc[:, h * RG_HEAD_DIM:(h + 1) * RG_HEAD_DIM]
        rs.append(jnp.dot(xh, wa_ref[h], precision=HIGHEST, preferred_element_type=f32))
        gs.append(jnp.dot(xh, wx_ref[h], precision=HIGHEST, preferred_element_type=f32))
    r = _sigmoid(jnp.concatenate(rs, axis=1) + ba_ref[...])
    g = _sigmoid(jnp.concatenate(gs, axis=1) + bx_ref[...])
    log_a = -RG_C * r * _softplus(-lam_ref[...])
    a = jnp.exp(log_a)
    u = jnp.sqrt(-jnp.tanh(log_a) * (a * a + 1.0)) * (g * xc)
    rowid = lax.broadcasted_iota(i32, (tr, 1), 0)
    s = 1
    while s < tr:
        if reverse:
            keep = rowid < tr - s
            shift = tr - s
        else:
            keep = rowid >= s
            shift = s
        a_sh = jnp.where(keep, pltpu.roll(a, shift, 0), 1.0)
        u_sh = jnp.where(keep, pltpu.roll(u, shift, 0), 0.0)
        u = u + a * u_sh
        a = a * a_sh
        s *= 2
    hseq = a * carry[0:1, :] + u
    o_ref[0] = hseq
    last = 0 if reverse else tr - 1
    carry[0:1, :] = hseq[last:last + 1, :]


def rglru_scan(xc, wa, wx, ba, bx, lam, nctx_blk, reverse):
    b, tt, c = xc.shape
    tr = ROW_BLOCK
    nblk = tt // tr
    if reverse:
        def blk(bi, i):
            return (bi, jnp.where(i < nctx_blk, nctx_blk - 1 - i, nblk - 1 - (i - nctx_blk)), 0)
    else:
        def blk(bi, i):
            return (bi, i, 0)
    vec = pl.BlockSpec((1, c), lambda bi, i: (0, 0))
    wspec = pl.BlockSpec((RG_HEADS, RG_HEAD_DIM, RG_HEAD_DIM), lambda bi, i: (0, 0, 0))
    return pl.pallas_call(
        functools.partial(_rglru_kernel, reverse),
        grid=(b, nblk),
        in_specs=[pl.BlockSpec((1, tr, c), blk), wspec, wspec, vec, vec, vec],
        out_specs=pl.BlockSpec((1, tr, c), blk),
        out_shape=jax.ShapeDtypeStruct((b, tt, c), f32),
        scratch_shapes=[pltpu.VMEM((SUBLANES, c), f32)],
        compiler_params=_cparams(("arbitrary", "arbitrary")),
        name="rglru_rev" if reverse else "rglru_fwd",
    )(xc, wa, wx, ba.reshape(1, c), bx.reshape(1, c), lam.reshape(1, c))


def _dft_kernel(t_len, x_ref, o_ref, c2, s2):
    mi = pl.program_id(0)
    ki = pl.program_id(1)
    bm, bk = c2.shape
    w0 = 2.0 * math.pi / t_len

    @pl.when(ki == 0)
    def _():
        k = mi * bm + lax.broadcasted_iota(i32, (bm, bk), 0)
        a = lax.broadcasted_iota(i32, (bm, bk), 1)
        ang = ((k * a) & (t_len - 1)).astype(f32) * w0
        c2[...] = jnp.cos(ang)
        s2[...] = jnp.sin(ang)
        o_ref[...] = jnp.zeros_like(o_ref)

    kcol = mi * bm + lax.broadcasted_iota(i32, (bm, 1), 0)
    ang1 = ((kcol * (ki * bk)) & (t_len - 1)).astype(f32) * w0
    c1 = jnp.cos(ang1)
    s1 = jnp.sin(ang1)
    cm = (c1 * c2[...] - s1 * s2[...]).astype(bf16)
    sm = (s1 * c2[...] + c1 * s2[...]).astype(bf16)
    w = x_ref.shape[2]
    for b in range(x_ref.shape[0]):
        xb = x_ref[b].astype(bf16)
        o_ref[b, :, 0:w] += jnp.dot(cm, xb, preferred_element_type=f32)
        o_ref[b, :, w:2 * w] += jnp.dot(sm, xb, preferred_element_type=f32)


def dft_positions(p, col_blk, width, row0, t_len):
    b = p.shape[0]
    assert t_len & (t_len - 1) == 0
    bk = min(ROW_BLOCK, t_len)
    bm = min(1024, t_len)
    r0 = row0 // bk
    return pl.pallas_call(
        functools.partial(_dft_kernel, t_len),
        grid=(t_len // bm, t_len // bk),
        in_specs=[pl.BlockSpec((b, bk, width), lambda mi, ki: (0, r0 + ki, col_blk))],
        out_specs=pl.BlockSpec((b, bm, 2 * width), lambda mi, ki: (0, mi, 0)),
        out_shape=jax.ShapeDtypeStruct((b, t_len, 2 * width), f32),
        scratch_shapes=[pltpu.VMEM((bm, bk), f32), pltpu.VMEM((bm, bk), f32)],
        compiler_params=_cparams(("arbitrary", "arbitrary")),
        name="dft_positions",
    )(p)


def _residual_and_ffn_norm(h, y, mod_ref, gf_ref, hout_ref, f_ref):
    hn = h + mod_ref[0, 2:3, :] * y
    hout_ref[0] = hn
    f_ref[0] = _rms(hn, gf_ref[...]) * (1.0 + mod_ref[0, 4:5, :]) + mod_ref[0, 3:4, :]


def _merge_even_kernel(nctx_blk, sc_ctx, sc_lat, hf_ref, hb_ref, gate_ref, z_ref, h_ref, mod_ref, cc_ref, cs_ref,
                       fw_ref, fb_ref, wa_ref, wb_ref, gf_ref, hout_ref, f_ref):
    i = pl.program_id(1)
    ya = (hf_ref[0] + hb_ref[0]) * _gelu_tanh(gate_ref[0])
    z = z_ref[0]
    w = z.shape[1] // 2
    zz = (jnp.dot(z[:, :w], cc_ref[...], precision=HIGHEST, preferred_element_type=f32)
          - jnp.dot(z[:, w:], cs_ref[...], precision=HIGHEST, preferred_element_type=f32))
    zz = zz * jnp.where(i < nctx_blk, sc_ctx, sc_lat)
    yb = jnp.dot(zz, fw_ref[...], precision=HIGHEST, preferred_element_type=f32) + fb_ref[...]
    y = (jnp.dot(ya.astype(bf16), wa_ref[...], preferred_element_type=f32)
         + jnp.dot(yb.astype(bf16), wb_ref[...], preferred_element_type=f32))
    _residual_and_ffn_norm(h_ref[0], y, mod_ref, gf_ref, hout_ref, f_ref)


def merge_even(hf, hb, p, z, h, mod, fw_bd, fb, w_out, gf, nctx_blk, t_ctx, t_lat):
    b, tt, d = h.shape
    tr = ROW_BLOCK
    jj, kk = np.meshgrid(np.arange(FN_GROUP_DIM), np.arange(FN_GROUP_DIM), indexing="ij")
    ang = 2.0 * np.pi * jj * kk / FN_GROUP_DIM
    eye = np.eye(FN_GROUPS)
    cc = jnp.asarray(np.kron(eye, np.cos(ang)), f32)
    cs = jnp.asarray(np.kron(eye, np.sin(ang)), f32)
    sc_ctx = 1.0 / math.sqrt(t_ctx * FN_GROUP_DIM)
    sc_lat = 1.0 / math.sqrt(t_lat * FN_GROUP_DIM)

    def rows(wd, cb=0):
        return pl.BlockSpec((1, tr, wd), lambda bi, i: (bi, i, cb))

    def full(shape):
        return pl.BlockSpec(shape, lambda bi, i: (0,) * len(shape))

    return pl.pallas_call(
        functools.partial(_merge_even_kernel, nctx_blk, sc_ctx, sc_lat),
        grid=(b, tt // tr),
        in_specs=[
            rows(RG_WIDTH), rows(RG_WIDTH), rows(RG_WIDTH, 1), rows(2 * FN_WIDTH), rows(d),
            pl.BlockSpec((1, SUBLANES, d), lambda bi, i: (_mod_row(bi, i, nctx_blk), 0, 0)),
            full((FN_WIDTH, FN_WIDTH)), full((FN_WIDTH, FN_WIDTH)), full((FN_WIDTH, FN_WIDTH)), full((1, FN_WIDTH)),
            full((RG_WIDTH, d)), full((FN_WIDTH, d)), full((1, d)),
        ],
        out_specs=[rows(d), rows(d)],
        out_shape=[jax.ShapeDtypeStruct((b, tt, d), f32)] * 2,
        compiler_params=_cparams(("arbitrary", "arbitrary")),
        name="merge_even",
    )(hf, hb, p, z, h, mod, cc, cs, fw_bd, fb.reshape(1, FN_WIDTH),
      w_out[:RG_WIDTH].astype(bf16), w_out[RG_WIDTH:].astype(bf16), gf.reshape(1, d))


def _merge_odd_kernel(hs0_ref, hs1_ref, o_ref, at_ref, h_ref, mod_ref, hg_ref, wm_ref, wa_ref, gf_ref, hout_ref, f_ref):
    hs = hs0_ref[0, 0] + hs1_ref[0, 0]
    parts = []
    for hd in range(ML_HEADS):
        sl = slice(hd * ML_HEAD_DIM, (hd + 1) * ML_HEAD_DIM)
        parts.append(_rms(hs[:, sl], hg_ref[:, sl]))
    yml = _sigmoid(o_ref[0]) * jnp.concatenate(parts, axis=1)
    y = (jnp.dot(yml.astype(bf16), wm_ref[...], preferred_element_type=f32)
         + jnp.dot(at_ref[0].astype(bf16), wa_ref[...], preferred_element_type=f32))
    _residual_and_ffn_norm(h_ref[0], y, mod_ref, gf_ref, hout_ref, f_ref)


def merge_odd(hs, p, attn, h, mod, head_g, w_ml, w_at, gf, nctx_blk):
    b, t, ap = attn.shape
    d = h.shape[2]
    tr = ROW_BLOCK

    def rows(wd, cb=0, off=0):
        return pl.BlockSpec((1, tr, wd), lambda bi, i: (bi, i + off, cb))

    def full(shape):
        return pl.BlockSpec(shape, lambda bi, i: (0,) * len(shape))

    return pl.pallas_call(
        _merge_odd_kernel,
        grid=(b, t // tr),
        in_specs=[
            pl.BlockSpec((1, 1, tr, ML_WIDTH), lambda bi, i: (0, bi, i + nctx_blk, 0)),
            pl.BlockSpec((1, 1, tr, ML_WIDTH), lambda bi, i: (1, bi, i + nctx_blk, 0)),
            rows(ML_WIDTH, OD_O // ML_WIDTH, nctx_blk), rows(ap), rows(d, 0, nctx_blk),
            pl.BlockSpec((1, SUBLANES, d), lambda bi, i: (bi, 0, 0)),
            full((1, ML_WIDTH)), full((ML_WIDTH, d)), full((ap, d)), full((1, d)),
        ],
        out_specs=[rows(d), rows(d)],
        out_shape=[jax.ShapeDtypeStruct((b, t, d), f32)] * 2,
        compiler_params=_cparams(("arbitrary", "arbitrary")),
        name="merge_odd",
    )(hs, hs, p, attn, h, mod, head_g.reshape(1, ML_WIDTH), w_ml, w_at, gf.reshape(1, d))


def _mlstm_kernel(qk_ref, v_ref, g_ref, gt_ref, gb_ref, gbt_ref, o_ref, ct, nst, mst):
    d = pl.program_id(0)
    c = pl.program_id(2)
    L = ML_CHUNK

    @pl.when(c == 0)
    def _():
        ct[...] = jnp.zeros_like(ct)
        nst[...] = jnp.zeros_like(nst)
        mst[...] = jnp.zeros_like(mst)

    ri = lax.broadcasted_iota(i32, (L, L), 0)
    ci = lax.broadcasted_iota(i32, (L, L), 1)
    sgn = 1 - 2 * d
    tri = (ci - ri) * sgn <= 0
    trit = (ri - ci) * sgn <= 0
    gcol = g_ref[0] + gb_ref[...]
    grow = gt_ref[0] + gbt_ref[...]
    lane = lax.broadcasted_iota(i32, (1, LANES), 1)
    for hd in range(ML_HEADS):
        sl = slice(hd * ML_HEAD_DIM, (hd + 1) * ML_HEAD_DIM)
        q = qk_ref[0, :, sl] * (ML_HEAD_DIM ** -0.5)
        k = qk_ref[0, :, ML_WIDTH + hd * ML_HEAD_DIM:ML_WIDTH + (hd + 1) * ML_HEAD_DIM]
        v = v_ref[0, :, sl]
        li = d * 8 + hd
        ig_col = jnp.sum(jnp.where(lane == li, gcol, 0.0), axis=1, keepdims=True)
        fg_col = jnp.sum(jnp.where(lane == li + 4, gcol, 0.0), axis=1, keepdims=True)
        rsel = lax.broadcasted_iota(i32, (ML_N_GATES, 1), 0)
        ig_row = jnp.sum(jnp.where(rsel == li, grow, 0.0), axis=0, keepdims=True)
        fg_row = jnp.sum(jnp.where(rsel == li + 4, grow, 0.0), axis=0, keepdims=True)
        lf_col = -_softplus(-fg_col)
        lf_row = -_softplus(-fg_row)
        b_col = jnp.sum(jnp.where(tri, lf_row, 0.0), axis=1, keepdims=True)
        b_row = jnp.sum(jnp.where(trit, lf_col, 0.0), axis=0, keepdims=True)
        m_prev = mst[hd, 0:1, 0:1]
        n_prev = nst[hd, 0:1, :]
        ct_prev = ct[hd]
        dm = jnp.where(tri, b_col - b_row + ig_row, -jnp.inf)
        inter = b_col + m_prev
        m_row = jnp.maximum(jnp.max(dm, axis=1, keepdims=True), inter)
        qb, kb, vb = q.astype(bf16), k.astype(bf16), v.astype(bf16)
        s = lax.dot_general(qb, kb, (((1,), (1,)), ((), ())), preferred_element_type=f32) * jnp.exp(dm - m_row)
        w_inter = jnp.exp(inter - m_row)
        num = (jnp.dot(s.astype(bf16), vb, preferred_element_type=f32)
               + w_inter * jnp.dot(qb, ct_prev.astype(bf16), preferred_element_type=f32))
        den = jnp.sum(s, axis=1, keepdims=True) + w_inter * jnp.sum(q * n_prev, axis=1, keepdims=True)
        o_ref[0, 0, :, sl] = num / jnp.maximum(jnp.abs(den), jnp.exp(-m_row))
        b_l = jnp.sum(lf_col, axis=0, keepdims=True)
        log_wk = b_l - b_col + ig_col
        m_new = jnp.maximum(b_l + m_prev, jnp.max(log_wk, axis=0, keepdims=True))
        kw = jnp.exp(log_wk - m_new) * k
        decay = jnp.exp(b_l + m_prev - m_new)
        ct[hd] = decay * ct_prev + lax.dot_general(kw.astype(bf16), vb, (((0,), (0,)), ((), ())),
                                                   preferred_element_type=f32)
        nst[hd, 0:1, :] = decay * n_prev + jnp.sum(kw, axis=0, keepdims=True)
        mst[hd] = jnp.broadcast_to(m_new, (SUBLANES, LANES))


def mlstm(qkc, p, gt, gate_b, nctx_chunks):
    b, tt, _ = qkc.shape
    L = ML_CHUNK
    nch = tt // L

    def chunk(d, c):
        return jnp.where(d == 0, c, jnp.where(c < nctx_chunks, nctx_chunks - 1 - c, nch - 1 - (c - nctx_chunks)))

    gb = jnp.zeros((1, LANES), f32).at[0, :ML_N_GATES].set(gate_b)
    return pl.pallas_call(
        _mlstm_kernel,
        grid=(2, b, nch),
        in_specs=[
            pl.BlockSpec((1, L, 2 * ML_WIDTH), lambda d, bi, c: (bi, chunk(d, c), 0)),
            pl.BlockSpec((1, L, ML_WIDTH), lambda d, bi, c: (bi, chunk(d, c), OD_V // ML_WIDTH)),
            pl.BlockSpec((1, L, LANES), lambda d, bi, c: (bi, chunk(d, c), OD_G // LANES)),
            pl.BlockSpec((1, ML_N_GATES, L), lambda d, bi, c: (bi, 0, chunk(d, c))),
            pl.BlockSpec((1, LANES), lambda d, bi, c: (0, 0)),
            pl.BlockSpec((ML_N_GATES, 1), lambda d, bi, c: (0, 0)),
        ],
        out_specs=pl.BlockSpec((1, 1, L, ML_WIDTH), lambda d, bi, c: (d, bi, chunk(d, c), 0)),
        out_shape=jax.ShapeDtypeStruct((2, b, tt, ML_WIDTH), f32),
        scratch_shapes=[
            pltpu.VMEM((ML_HEADS, ML_HEAD_DIM, ML_HEAD_DIM), f32),
            pltpu.VMEM((ML_HEADS, SUBLANES, LANES), f32),
            pltpu.VMEM((ML_HEADS, SUBLANES, LANES), f32),
        ],
        compiler_params=_cparams(("arbitrary", "arbitrary", "arbitrary")),
        name="mlstm",
    )(qkc, p, p, gt, gb, gate_b.reshape(ML_N_GATES, 1))


def _mla_prep_kernel(nctx_blk, p_ref, qg_ref, kg_ref, wq_ref, wqs_ref, wk_ref, wv_ref, em_ref, es_ref,
                     q_ref, k_ref, v_ref):
    i = pl.program_id(1)
    tr = p_ref.shape[1]
    cq = _rms(p_ref[0, :, OD_CQ:OD_CKV], qg_ref[...]).astype(bf16)
    ckv = _rms(p_ref[0, :, OD_CKV:OD_KR], kg_ref[...]).astype(bf16)
    kr = p_ref[0, :, OD_KR:OD_N]
    lane = lax.broadcasted_iota(i32, (1, MLA_HEAD_PAD), 1)
    half = MLA_ROPE // 2
    pair = jnp.where(lane < MLA_NOPE + half, lane - MLA_NOPE, lane - MLA_NOPE - half)
    is_rope = jnp.logical_and(lane >= MLA_NOPE, lane < MLA_NOPE + MLA_ROPE)
    use_row = pair < half // 2
    fidx = jnp.clip(jnp.where(use_row, pair, pair - half // 2), 0, half // 2 - 1).astype(f32)
    inv = jnp.exp(fidx * (-math.log(ROPE_THETA) / (half // 2)))
    t = (i - nctx_blk) * tr + lax.broadcasted_iota(i32, (tr, 1), 0)
    pos = jnp.where(use_row, (t >> GRID_SHIFT).astype(f32), (t & (GRID_W - 1)).astype(f32))
    ang = pos * inv
    rot = jnp.logical_and(is_rope, i >= nctx_blk)
    cos1 = jnp.where(rot, jnp.cos(ang), 1.0)
    sin1 = jnp.where(rot, jnp.sin(ang), 0.0)
    cos_t = jnp.concatenate([cos1] * MLA_HEADS, axis=1)
    sin_t = jnp.concatenate([sin1] * MLA_HEADS, axis=1)
    scale = (MLA_NOPE + MLA_ROPE) ** -0.5
    q = (jnp.dot(cq, wq_ref[...], preferred_element_type=f32) * cos_t
         + jnp.dot(cq, wqs_ref[...], preferred_element_type=f32) * sin_t)
    q_ref[0] = (q * scale).astype(bf16)
    k = (jnp.dot(ckv, wk_ref[...], preferred_element_type=f32)
         + jnp.dot(kr, em_ref[...], precision=HIGHEST, preferred_element_type=f32) * cos_t
         + jnp.dot(kr, es_ref[...], precision=HIGHEST, preferred_element_type=f32) * sin_t)
    k_ref[0] = k.astype(bf16)
    v_ref[0] = jnp.dot(ckv, wv_ref[...], preferred_element_type=f32).astype(bf16)


def _mla_weights(w_uq, w_ukv):
    hp = MLA_HEAD_PAD
    half = MLA_ROPE // 2
    wq = w_uq.reshape(MLA_Q_LORA, MLA_HEADS, MLA_NOPE + MLA_ROPE)
    nope, x1, x2 = wq[..., :MLA_NOPE], wq[..., MLA_NOPE::2], wq[..., MLA_NOPE + 1::2]
    zq = jnp.zeros((MLA_Q_LORA, MLA_HEADS, hp - MLA_NOPE - MLA_ROPE), f32)
    wq_main = jnp.concatenate([nope, x1, x2, zq], axis=-1).reshape(MLA_Q_LORA, MLA_HEADS * hp)
    wq_swap = jnp.concatenate([jnp.zeros_like(nope), -x2, x1, zq], axis=-1).reshape(MLA_Q_LORA, MLA_HEADS * hp)
    wkv = w_ukv.reshape(MLA_KV_LORA, MLA_HEADS, MLA_NOPE + MLA_V)
    zk = jnp.zeros((MLA_KV_LORA, MLA_HEADS, hp - MLA_NOPE), f32)
    wk = jnp.concatenate([wkv[..., :MLA_NOPE], zk], axis=-1).reshape(MLA_KV_LORA, MLA_HEADS * hp)
    wv = jnp.concatenate([wkv[..., MLA_NOPE:], jnp.zeros((MLA_KV_LORA, MLA_HEADS, hp - MLA_V), f32)],
                         axis=-1).reshape(MLA_KV_LORA, MLA_HEADS * hp)
    em = np.zeros((LANES, MLA_HEADS, hp), np.float32)
    es = np.zeros((LANES, MLA_HEADS, hp), np.float32)
    for j in range(half):
        em[2 * j, :, MLA_NOPE + j] = 1.0
        em[2 * j + 1, :, MLA_NOPE + half + j] = 1.0
        es[2 * j + 1, :, MLA_NOPE + j] = -1.0
        es[2 * j, :, MLA_NOPE + half + j] = 1.0
    em = jnp.asarray(em.reshape(LANES, MLA_HEADS * hp))
    es = jnp.asarray(es.reshape(LANES, MLA_HEADS * hp))
    return wq_main.astype(bf16), wq_swap.astype(bf16), wk.astype(bf16), wv.astype(bf16), em, es


def mla_prep(p, q_norm_g, kv_norm_g, w_uq, w_ukv, nctx_blk):
    b, tt, n = p.shape
    tr = ROW_BLOCK
    hw = MLA_HEADS * MLA_HEAD_PAD
    wq, wqs, wk, wv, em, es = _mla_weights(w_uq, w_ukv)

    def full(shape):
        return pl.BlockSpec(shape, lambda bi, i: (0,) * len(shape))

    out = pl.BlockSpec((1, tr, hw), lambda bi, i: (bi, i, 0))
    return pl.pallas_call(
        functools.partial(_mla_prep_kernel, nctx_blk),
        grid=(b, tt // tr),
        in_specs=[pl.BlockSpec((1, tr, n), lambda bi, i: (bi, i, 0)),
                  full((1, MLA_Q_LORA)), full((1, MLA_KV_LORA)),
                  full((MLA_Q_LORA, hw)), full((MLA_Q_LORA, hw)), full((MLA_KV_LORA, hw)), full((MLA_KV_LORA, hw)),
                  full((LANES, hw)), full((LANES, hw))],
        out_specs=[out, out, out],
        out_shape=[jax.ShapeDtypeStruct((b, tt, hw), bf16)] * 3,
        compiler_params=_cparams(("arbitrary", "arbitrary")),
        name="mla_prep",
    )(p, q_norm_g.reshape(1, MLA_Q_LORA), kv_norm_g.reshape(1, MLA_KV_LORA), wq, wqs, wk, wv, em, es)


def _mla_attn_kernel(q_ref, k_ref, v_ref, o_ref):
    s = lax.dot_general(q_ref[0], k_ref[0], (((1,), (1,)), ((), ())), preferred_element_type=f32)
    m = jnp.max(s, axis=-1, keepdims=True)
    e = jnp.exp(s - m)
    l = jnp.sum(e, axis=-1, keepdims=True)
    o_ref[0] = jnp.dot(e.astype(bf16), v_ref[0], preferred_element_type=f32) / l


def mla_attention(q, k, v, nctx_blk):
    b, tt, hw = q.shape
    tq = ROW_BLOCK
    nq = tt // tq - nctx_blk
    hp = MLA_HEAD_PAD
    kv = pl.BlockSpec((1, tt, hp), lambda bi, h, qi: (bi, 0, h))
    return pl.pallas_call(
        _mla_attn_kernel,
        grid=(b, MLA_HEADS, nq),
        in_specs=[pl.BlockSpec((1, tq, hp), lambda bi, h, qi: (bi, qi + nctx_blk, h)), kv, kv],
        out_specs=pl.BlockSpec((1, tq, hp), lambda bi, h, qi: (bi, qi, h)),
        out_shape=jax.ShapeDtypeStruct((b, nq * tq, hw), f32),
        compiler_params=_cparams(("arbitrary", "arbitrary", "arbitrary")),
        name="mla_attention",
    )(q, k, v)


def _topk_rows(s, k, payload=None):
    n = s.shape[0]
    rid = lax.broadcasted_iota(i32, s.shape, 0)
    vals, outs = [], []
    for _ in range(k):
        m = jnp.max(s, axis=0, keepdims=True)
        am = jnp.min(jnp.where(s == m, rid, n), axis=0, keepdims=True)
        hit = rid == am
        vals.append(m)
        outs.append(am if payload is None else jnp.sum(jnp.where(hit, payload, 0), axis=0, keepdims=True))
        s = jnp.where(hit, -jnp.inf, s)
    return jnp.concatenate(vals, axis=0), jnp.concatenate(outs, axis=0)


def _pruned_candidates(vals, idxs):
    k = PEER_TOPK
    hk = k // 2
    bcol = lax.broadcasted_iota(i32, (hk, 1), 0)
    cand = [vals[0][0:1, :] + vals[1]]
    cidx = [idxs[0][0:1, :] * PEER_N_KEYS + idxs[1]]
    for a in range(1, hk):
        ok = (a + 1) * (bcol + 1) <= k
        cand.append(jnp.where(ok, vals[0][a:a + 1, :] + vals[1][0:hk, :], -jnp.inf))
        cidx.append(idxs[0][a:a + 1, :] * PEER_N_KEYS + idxs[1][0:hk, :])
    cand.append(vals[0][hk:k, :] + vals[1][0:1, :])
    cidx.append(idxs[0][hk:k, :] * PEER_N_KEYS + idxs[1][0:1, :])
    return jnp.concatenate(cand, axis=0), jnp.concatenate(cidx, axis=0)


def _peer_route_kernel(f_ref, wq_ref, keys_ref, code_ref, g_ref):
    q = jnp.dot(f_ref[...].astype(bf16), wq_ref[...], preferred_element_type=f32)
    hk = PEER_KEY_DIM // 2
    for h in range(PEER_HEADS):
        vals, idxs = [], []
        for p in range(2):
            c0 = (h * 2 + p) * hk
            qhp = q[:, c0:c0 + hk].astype(bf16)
            st = lax.dot_general(keys_ref[p, h], qhp, (((1,), (1,)), ((), ())), preferred_element_type=f32)
            v, ix = _topk_rows(st, PEER_TOPK)
            vals.append(v)
            idxs.append(ix)
        cand, cidx = _pruned_candidates(vals, idxs)
        top_s, e = _topk_rows(cand, PEER_TOPK, payload=cidx)
        ex = jnp.exp(top_s - top_s[0:1, :])
        rows = slice(h * PEER_TOPK, (h + 1) * PEER_TOPK)
        g_ref[rows, :] = ex / jnp.sum(ex, axis=0, keepdims=True)
        code_ref[rows, :] = (e >> 1) * SUBLANES + ((1 - (e & 1)) << LOW_HALF_BIT)


def peer_route(f2, wq, keys):
    r, d = f2.shape
    tb = PEER_TB
    out = pl.BlockSpec((PEER_PICKS, tb), lambda i: (0, i))
    return pl.pallas_call(
        _peer_route_kernel,
        grid=(r // tb,),
        in_specs=[
            pl.BlockSpec((tb, d), lambda i: (i, 0)),
            pl.BlockSpec(wq.shape, lambda i: (0, 0)),
            pl.BlockSpec(keys.shape, lambda i: (0, 0, 0, 0)),
        ],
        out_specs=[out, out],
        out_shape=[jax.ShapeDtypeStruct((PEER_PICKS, r), i32), jax.ShapeDtypeStruct((PEER_PICKS, r), f32)],
        compiler_params=_cparams(("arbitrary",)),
        name="peer_route",
    )(f2, wq, keys)


def pack_expert_table(t):
    e, d = t.shape
    bits = lax.bitcast_convert_type(t.astype(bf16), jnp.uint16).astype(jnp.uint32)
    bits = bits.reshape(e // 2, 2, SUBLANES, d // SUBLANES)
    word = bits[:, 0] | (bits[:, 1] << 16)
    return lax.bitcast_convert_type(word, i32).reshape(e // 2 * SUBLANES, d // SUBLANES)


def _load_table_once(tab_hbm, tab_vmem, sem):
    @pl.when(pl.program_id(0) == 0)
    def _():
        cp = pltpu.make_async_copy(tab_hbm, tab_vmem, sem)
        cp.start()
        cp.wait()


def _slot_major(a, tb):
    r = a.shape[0]
    ng = PEER_PICKS // SUBLANES
    return a.reshape(r // tb, tb, ng, SUBLANES).transpose(0, 3, 1, 2).reshape(r // tb, SUBLANES, tb * ng)


def _fetch_block(src_hbm, dst_smems, sems, i):
    cps = [pltpu.make_async_copy(src_hbm.at[i, s], dst_smems[s], sems.at[s]) for s in range(SUBLANES)]
    for cp in cps:
        cp.start()
    for cp in cps:
        cp.wait()


def _rotate_tiles(tb, mbufs, row_s, tab_vmem, consume):
    nb = len(mbufs)
    ng = PEER_PICKS // SUBLANES

    def copy_tiles(t, slot):
        for jg in range(ng):
            g = t * ng + jg
            for s in range(SUBLANES):
                row = pl.multiple_of(row_s[s][g], SUBLANES)
                mbufs[slot][pl.ds((jg * SUBLANES + s) * SUBLANES, SUBLANES), :] = tab_vmem[pl.ds(row, SUBLANES), :]

    def rotate(p, c):
        for q in range(nb):
            consume(p * nb + q, mbufs[q])
            copy_tiles((p + 1) * nb + q, q)
        return c

    for q in range(nb):
        copy_tiles(q, q)
    lax.fori_loop(0, tb // nb - 1, rotate, 0)
    for q in range(nb):
        consume(tb - nb + q, mbufs[q])


def _peer_act_kernel(row_hbm, code_ref, x_ref, g_ref, tab_hbm, ex_ref, rd_ref, wx_ref, tab_vmem, abuf, sem_tab, sem_a,
                     *bufs):
    i = pl.program_id(0)
    tb = x_ref.shape[0]
    nr = 2 * SUBLANES
    nx = PEER_PICKS * nr
    nb = PEER_TILE_DEPTH
    mbufs, row_s = bufs[:nb], bufs[nb:]
    _load_table_once(tab_hbm, tab_vmem, sem_tab)
    _fetch_block(row_hbm, row_s, sem_a, i)
    lane = lax.broadcasted_iota(i32, (nr, nx), 1)
    sub = lax.broadcasted_iota(i32, (nr, nx), 0)
    own = (lane & (nr - 1)) == sub

    def dots(t, mbuf):
        gm = lax.dot_general(x_ref[t], pltpu.bitcast(mbuf[...], bf16), (((1,), (1,)), ((), ())),
                             preferred_element_type=f32)
        abuf[pl.ds(t, 1), :] = jnp.sum(jnp.where(own, gm, 0.0), axis=0, keepdims=True)

    _rotate_tiles(tb, mbufs, row_s, tab_vmem, dots)
    a = abuf[...]
    a_hi = a.astype(bf16)
    a_lo = (a - a_hi.astype(f32)).astype(bf16)
    low = ((code_ref[...] >> LOW_HALF_BIT) & 1) == 1
    act = jnp.where(
        low,
        jnp.dot(a_hi, rd_ref[0], preferred_element_type=f32) + jnp.dot(a_lo, rd_ref[0], preferred_element_type=f32),
        jnp.dot(a_hi, rd_ref[1], preferred_element_type=f32) + jnp.dot(a_lo, rd_ref[1], preferred_element_type=f32))
    w = g_ref[...] * _gelu_tanh(act)
    wx_ref[...] = (jnp.dot(jnp.where(low, w, 0.0).astype(bf16), ex_ref[0], preferred_element_type=f32)
                   + jnp.dot(jnp.where(low, 0.0, w).astype(bf16), ex_ref[1], preferred_element_type=f32))


def _expansion_matrices():
    ex = np.zeros((2, PEER_PICKS, PEER_PICKS * 2 * SUBLANES), np.float32)
    for j in range(PEER_PICKS):
        for s in range(SUBLANES):
            ex[0, j, j * 2 * SUBLANES + 2 * s] = 1.0
            ex[1, j, j * 2 * SUBLANES + 2 * s + 1] = 1.0
    return ex


def peer_act(row_sm, code, x2, g, tab):
    r = x2.shape[0]
    tb = PEER_TB
    ng = PEER_PICKS // SUBLANES
    nx = PEER_PICKS * 2 * SUBLANES
    ex = _expansion_matrices()
    anyspec = pl.BlockSpec(memory_space=pl.ANY)
    rows = pl.BlockSpec((tb, PEER_PICKS), lambda i: (i, 0))
    return pl.pallas_call(
        _peer_act_kernel,
        grid=(r // tb,),
        in_specs=[anyspec, rows, pl.BlockSpec((tb, 2 * SUBLANES, LANES), lambda i: (i, 0, 0)), rows, anyspec,
                  pl.BlockSpec((2, PEER_PICKS, nx), lambda i: (0, 0, 0)),
                  pl.BlockSpec((2, nx, PEER_PICKS), lambda i: (0, 0, 0))],
        out_specs=pl.BlockSpec((tb, nx), lambda i: (i, 0)),
        out_shape=jax.ShapeDtypeStruct((r, nx), f32),
        scratch_shapes=[
            pltpu.VMEM(tab.shape, i32),
            pltpu.VMEM((tb, nx), f32),
            pltpu.SemaphoreType.DMA, pltpu.SemaphoreType.DMA((SUBLANES,)),
        ] + [pltpu.VMEM((PEER_PICKS * SUBLANES, LANES), i32)] * PEER_TILE_DEPTH
          + [pltpu.SMEM((tb * ng,), i32)] * SUBLANES,
        compiler_params=_cparams(("arbitrary",), VMEM_LIMIT_TABLE),
        name="peer_act",
    )(row_sm, code, x2, g, tab, jnp.asarray(ex, bf16), jnp.asarray(ex.transpose(0, 2, 1), bf16))


def _peer_out_kernel(row_hbm, wx_ref, tab_hbm, o_ref, tab_vmem, sem_tab, sem_a, *bufs):
    i = pl.program_id(0)
    tb = o_ref.shape[0]
    nx = wx_ref.shape[1]
    nb = PEER_TILE_DEPTH
    mbufs, row_s = bufs[:nb], bufs[nb:]
    _load_table_once(tab_hbm, tab_vmem, sem_tab)
    _fetch_block(row_hbm, row_s, sem_a, i)
    lane = lax.broadcasted_iota(i32, (SUBLANES, nx), 1)
    sub = lax.broadcasted_iota(i32, (SUBLANES, nx), 0)
    diag = ((lane & (2 * SUBLANES - 1)) >> 1) == sub

    def contract(t, mbuf):
        wsel = jnp.where(diag, jnp.broadcast_to(wx_ref[pl.ds(t, 1), :], (SUBLANES, nx)), 0.0).astype(bf16)
        o_ref[t] = jnp.dot(wsel, pltpu.bitcast(mbuf[...], bf16), preferred_element_type=f32)

    _rotate_tiles(tb, mbufs, row_s, tab_vmem, contract)


def peer_out(row_sm, wx, tab):
    nblk, _, n = row_sm.shape
    ng = PEER_PICKS // SUBLANES
    tb = n // ng
    nx = wx.shape[1]
    assert tb % PEER_TILE_DEPTH == 0
    anyspec = pl.BlockSpec(memory_space=pl.ANY)
    return pl.pallas_call(
        _peer_out_kernel,
        grid=(nblk,),
        in_specs=[anyspec, pl.BlockSpec((tb, nx), lambda i: (i, 0)), anyspec],
        out_specs=pl.BlockSpec((tb, SUBLANES, LANES), lambda i: (i, 0, 0)),
        out_shape=jax.ShapeDtypeStruct((nblk * tb, SUBLANES, LANES), f32),
        scratch_shapes=[
            pltpu.VMEM(tab.shape, i32),
            pltpu.SemaphoreType.DMA, pltpu.SemaphoreType.DMA((SUBLANES,)),
        ] + [pltpu.VMEM((PEER_PICKS * SUBLANES, LANES), i32)] * PEER_TILE_DEPTH + [pltpu.SMEM((n,), i32)] * SUBLANES,
        compiler_params=_cparams(("arbitrary",), VMEM_LIMIT_TABLE),
        name="peer_out",
    )(row_sm, wx, tab)


def peer_ffn(f, wq, keys, u, v):
    b, r, d = f.shape
    f2 = f.reshape(b * r, d)
    codet, gt = peer_route(f2, wq.astype(bf16), keys.astype(bf16))
    code = codet.T
    row_sm = _slot_major(code & ROW_MASK, PEER_TB)
    x2 = jnp.repeat(f2.reshape(b * r, SUBLANES, d // SUBLANES), 2, axis=1).astype(bf16)
    wx = peer_act(row_sm, code, x2, gt.T, pack_expert_table(u))
    o = peer_out(row_sm, wx, pack_expert_table(v))
    return o.reshape(b, r, d)


def _final_kernel(h_ref, peer_ref, mod_ref, g_ref, o_ref):
    h = h_ref[0] + mod_ref[0, 5:6, :] * peer_ref[0]
    o_ref[0] = _rms(h, g_ref[...])


def final_norm(h, peer, mod, g):
    b, t, d = h.shape
    tr = ROW_BLOCK
    row = pl.BlockSpec((1, tr, d), lambda bi, i: (bi, i, 0))
    return pl.pallas_call(
        _final_kernel,
        grid=(b, t // tr),
        in_specs=[row, row, pl.BlockSpec((1, SUBLANES, d), lambda bi, i: (bi, 0, 0)),
                  pl.BlockSpec((1, d), lambda bi, i: (0, 0))],
        out_specs=row,
        out_shape=jax.ShapeDtypeStruct((b, t, d), f32),
        compiler_params=_cparams(("arbitrary", "arbitrary")),
        name="final_norm",
    )(h, peer, mod, g.reshape(1, d))


def _block_diag(w):
    g, a, c = w.shape
    out = jnp.zeros((g * a, g * c), w.dtype)
    for k in range(g):
        out = out.at[k * a:(k + 1) * a, k * c:(k + 1) * c].set(w[k])
    return out


def kernel(x, c, ctx, c_ctx, mod_w, mod_b, norm_mix_g, norm_ffn_g, ev_w_in, ev_w_out, rg_conv_w, rg_conv_b, rg_wa, rg_ba, rg_wx, rg_bx, rg_lam, fn_w, fn_b, od_w_in, od_w_out, ml_conv_w, ml_conv_b, ml_gate_b, ml_head_g, mla_q_norm_g, mla_w_uq, mla_kv_norm_g, mla_w_ukv, peer_w_q, peer_keys, peer_u, peer_v, final_g):
    b, t, d = x.shape
    tc = ctx.shape[1]
    assert b == 2 and tc % ROW_BLOCK == 0 and t % ROW_BLOCK == 0 and mod_w.shape[0] == 2
    nctx_blk = tc // ROW_BLOCK

    cvec = jnp.zeros((SUBLANES, d), f32).at[0:b].set(c).at[b].set(c_ctx)
    mods = modulation(cvec, mod_w, mod_b)
    mods = mods.reshape(2, SUBLANES, 6, d)[:, :3]
    mods = jnp.pad(mods, ((0, 0), (0, 0), (0, 2), (0, 0)))

    h0 = jnp.concatenate([ctx, x], axis=1)

    p = inproj(h0, mods[0], norm_mix_g[0], ev_w_in[0].astype(bf16), nctx_blk)
    xc = dwconv(p, RG_WIDTH, rg_conv_w[0], rg_conv_b[0], nctx_blk, apply_silu=False)
    hf = rglru_scan(xc, rg_wa[0, 0], rg_wx[0, 0], rg_ba[0, 0], rg_bx[0, 0], rg_lam[0, 0], nctx_blk, reverse=False)
    hb = rglru_scan(xc, rg_wa[0, 1], rg_wx[0, 1], rg_ba[0, 1], rg_bx[0, 1], rg_lam[0, 1], nctx_blk, reverse=True)
    fcol = 2 * RG_WIDTH // FN_WIDTH
    z = jnp.concatenate([dft_positions(p, fcol, FN_WIDTH, 0, tc), dft_positions(p, fcol, FN_WIDTH, tc, t)], axis=1)
    h1, f1 = merge_even(hf, hb, p, z, h0, mods[0], _block_diag(fn_w[0]), fn_b[0], ev_w_out[0], norm_ffn_g[0],
                        nctx_blk, tc, t)
    peer1 = peer_ffn(f1, peer_w_q[0], peer_keys[0], peer_u[0], peer_v[0])

    w = od_w_in[0]
    zpad = lambda n: jnp.zeros((d, n), f32)
    w_od = jnp.concatenate([w[:, :OD_G + ML_N_GATES], zpad(OD_CQ - OD_G - ML_N_GATES),
                            w[:, 2064:2064 + MLA_Q_LORA + MLA_KV_LORA + MLA_ROPE],
                            zpad(OD_N - OD_KR - MLA_ROPE)], axis=1).astype(bf16)
    h2, p2 = inproj(h1, mods[1], norm_mix_g[1], w_od, nctx_blk, peer=peer1, mod_prev=mods[0])
    qkc = dwconv(p2, 2 * ML_WIDTH, ml_conv_w[0], ml_conv_b[0], nctx_blk, apply_silu=True)
    gt = jnp.swapaxes(p2[:, :, OD_G:OD_G + ML_N_GATES], 1, 2)
    hs = mlstm(qkc, p2, gt, ml_gate_b[0], tc // ML_CHUNK)
    q, k, v = mla_prep(p2, mla_q_norm_g[0], mla_kv_norm_g[0], mla_w_uq[0], mla_w_ukv[0], nctx_blk)
    attn = mla_attention(q, k, v, nctx_blk)
    wo = od_w_out[0]
    w_at = jnp.concatenate([wo[ML_WIDTH:].reshape(MLA_HEADS, MLA_V, d),
                            jnp.zeros((MLA_HEADS, MLA_HEAD_PAD - MLA_V, d), f32)], axis=1)
    w_at = w_at.reshape(MLA_HEADS * MLA_HEAD_PAD, d).astype(bf16)
    h3, f3 = merge_odd(hs, p2, attn, h2, mods[1, :b], ml_head_g[0], wo[:ML_WIDTH].astype(bf16), w_at,
                       norm_ffn_g[1], nctx_blk)
    peer3 = peer_ffn(f3, peer_w_q[1], peer_keys[1], peer_u[1], peer_v[1])
    return final_norm(h3, peer3, mods[1, :b], final_g)
```

```python
import functools
import math

import numpy as np
import jax
import jax.numpy as jnp
from jax import lax
from jax.experimental import pallas as pl
from jax.experimental.pallas import tpu as pltpu

f32 = jnp.float32
bf16 = jnp.bfloat16
i32 = jnp.int32
HIGHEST = lax.Precision.HIGHEST

SUBLANES = 8
LANES = 128
VMEM_LIMIT = 48 * 1024 * 1024
VMEM_LIMIT_TABLE = 56 * 1024 * 1024

EPS = 1e-6
GRID_W = 64
GRID_SHIFT = 6
ROW_BLOCK = 256
CONV_W = 4
RG_HEADS, RG_HEAD_DIM = 6, 128
RG_WIDTH = RG_HEADS * RG_HEAD_DIM
RG_C = 8.0
FN_GROUPS, FN_GROUP_DIM = 4, 64
FN_WIDTH = FN_GROUPS * FN_GROUP_DIM
ML_HEADS, ML_HEAD_DIM = 4, 128
ML_WIDTH = ML_HEADS * ML_HEAD_DIM
ML_CHUNK = 128
ML_N_GATES = 2 * 2 * ML_HEADS
MLA_HEADS, MLA_NOPE, MLA_ROPE, MLA_V = 8, 64, 32, 64
MLA_Q_LORA, MLA_KV_LORA = 384, 256
MLA_HEAD_PAD = 128
ATTN_Q_CHUNKS = 2
ROPE_THETA = 10000.0
PEER_HEADS, PEER_KEY_DIM, PEER_N_KEYS, PEER_TOPK = 8, 256, 128, 16
PEER_PICKS = PEER_HEADS * PEER_TOPK
PEER_TB = 128
PEER_TILE_DEPTH = 8
ROW_MASK = 0xFFF8
LOW_HALF_BIT = 20

OD_V, OD_O, OD_G, OD_CQ, OD_CKV, OD_KR, OD_N = 1024, 1536, 2048, 2176, 2560, 2816, 2944


def _cparams(sem, vmem=VMEM_LIMIT):
    return pltpu.CompilerParams(dimension_semantics=sem, vmem_limit_bytes=vmem)


def _gelu_tanh(x):
    return 0.5 * x * (1.0 + jnp.tanh(0.7978845608028654 * (x + 0.044715 * x * x * x)))


def _sigmoid(x):
    return 1.0 / (1.0 + jnp.exp(-x))


def _softplus(x):
    return jnp.maximum(x, 0.0) + jnp.log1p(jnp.exp(-jnp.abs(x)))


def _rms(x, g):
    return x * lax.rsqrt(jnp.mean(x * x, axis=-1, keepdims=True) + EPS) * g


def _mod_row(b, i, nctx_blk):
    return jnp.where(i < nctx_blk, 2, b)


def _mod_kernel(c_ref, w_ref, b_ref, o_ref):
    c = c_ref[...]
    s = c * _sigmoid(c)
    o_ref[0] = jnp.dot(s, w_ref[0], precision=HIGHEST, preferred_element_type=f32) + b_ref[0]


def modulation(cvec, mod_w, mod_b):
    depth, d, n = mod_w.shape
    tn = 1536
    return pl.pallas_call(
        _mod_kernel,
        grid=(depth, n // tn),
        in_specs=[
            pl.BlockSpec((SUBLANES, d), lambda l, j: (0, 0)),
            pl.BlockSpec((1, d, tn), lambda l, j: (l, 0, j)),
            pl.BlockSpec((1, 1, tn), lambda l, j: (l, 0, j)),
        ],
        out_specs=pl.BlockSpec((1, SUBLANES, tn), lambda l, j: (l, 0, j)),
        out_shape=jax.ShapeDtypeStruct((depth, SUBLANES, n), f32),
        compiler_params=_cparams(("arbitrary", "arbitrary")),
        name="modulation",
    )(cvec, mod_w, mod_b.reshape(depth, 1, n))


def _inproj_kernel(has_res, *refs):
    if has_res:
        h_ref, peer_ref, modp_ref, mod_ref, g_ref, w_ref, hout_ref, p_ref = refs
        h = h_ref[0] + modp_ref[0, 5:6, :] * peer_ref[0]
        hout_ref[0] = h
    else:
        h_ref, mod_ref, g_ref, w_ref, p_ref = refs
        h = h_ref[0]
    u = _rms(h, g_ref[...]) * (1.0 + mod_ref[0, 1:2, :]) + mod_ref[0, 0:1, :]
    p_ref[0] = jnp.dot(u.astype(bf16), w_ref[...], preferred_element_type=f32)


def inproj(h, mod, g, w, nctx_blk, peer=None, mod_prev=None):
    b, tt, d = h.shape
    n = w.shape[1]
    tr = ROW_BLOCK
    has_res = peer is not None
    row = pl.BlockSpec((1, tr, d), lambda bi, i: (bi, i, 0))
    modspec = pl.BlockSpec((1, SUBLANES, d), lambda bi, i: (_mod_row(bi, i, nctx_blk), 0, 0))
    in_specs = [row] + ([row, modspec] if has_res else []) + [
        modspec,
        pl.BlockSpec((1, d), lambda bi, i: (0, 0)),
        pl.BlockSpec((d, n), lambda bi, i: (0, 0)),
    ]
    pspec = pl.BlockSpec((1, tr, n), lambda bi, i: (bi, i, 0))
    pshape = jax.ShapeDtypeStruct((b, tt, n), f32)
    args = [h] + ([peer, mod_prev] if has_res else []) + [mod, g.reshape(1, d), w]
    return pl.pallas_call(
        functools.partial(_inproj_kernel, has_res),
        grid=(b, tt // tr),
        in_specs=in_specs,
        out_specs=[row, pspec] if has_res else pspec,
        out_shape=[jax.ShapeDtypeStruct((b, tt, d), f32), pshape] if has_res else pshape,
        compiler_params=_cparams(("arbitrary", "arbitrary")),
        name="inproj",
    )(*args)


def _conv_kernel(apply_silu, nctx_blk, nblk, prev_ref, cur_ref, next_ref, w_ref, b_ref, o_ref, ext):
    i = pl.program_id(1)
    tr = cur_ref.shape[1]
    has_prev = jnp.logical_and(i != 0, i != nctx_blk)
    has_next = jnp.logical_and(i != nctx_blk - 1, i != nblk - 1)
    ext[0:SUBLANES] = jnp.where(has_prev, prev_ref[0], 0.0)
    ext[SUBLANES:SUBLANES + tr] = cur_ref[0]
    ext[SUBLANES + tr:2 * SUBLANES + tr] = jnp.where(has_next, next_ref[0], 0.0)
    lo = CONV_W // 2
    y = b_ref[...] + w_ref[0:1, :] * ext[pl.ds(SUBLANES - lo, tr), :]
    for j in range(1, CONV_W):
        y = y + w_ref[j:j + 1, :] * ext[pl.ds(SUBLANES - lo + j, tr), :]
    if apply_silu:
        y = y * _sigmoid(y)
    o_ref[0] = y


def dwconv(p, c, w, bias, nctx_blk, apply_silu):
    b, tt, _ = p.shape
    tr = ROW_BLOCK
    nblk = tt // tr
    hb = tr // SUBLANES
    return pl.pallas_call(
        functools.partial(_conv_kernel, apply_silu, nctx_blk, nblk),
        grid=(b, nblk),
        in_specs=[
            pl.BlockSpec((1, SUBLANES, c), lambda bi, i: (bi, jnp.maximum(i * hb - 1, 0), 0)),
            pl.BlockSpec((1, tr, c), lambda bi, i: (bi, i, 0)),
            pl.BlockSpec((1, SUBLANES, c), lambda bi, i: (bi, jnp.minimum((i + 1) * hb, nblk * hb - 1), 0)),
            pl.BlockSpec((CONV_W, c), lambda bi, i: (0, 0)),
            pl.BlockSpec((1, c), lambda bi, i: (0, 0)),
        ],
        out_specs=pl.BlockSpec((1, tr, c), lambda bi, i: (bi, i, 0)),
        out_shape=jax.ShapeDtypeStruct((b, tt, c), f32),
        scratch_shapes=[pltpu.VMEM((tr + 2 * SUBLANES, c), f32)],
        compiler_params=_cparams(("arbitrary", "arbitrary")),
        name="dwconv",
    )(p, p, p, w, bias.reshape(1, c))


def _rglru_kernel(reverse, xc_ref, wa_ref, wx_ref, ba_ref, bx_ref, lam_ref, o_ref, carry):
    i = pl.program_id(1)

    @pl.when(i == 0)
    def _():
        carry[...] = jnp.zeros_like(carry)

    xc = xc_ref[0]
    tr = xc.shape[0]
    rs, gs = [], []
    for h in range(RG_HEADS):
        xh = xc[:, h * RG_HEAD_DIM:(h + 1) * RG_HEAD_DIM]
        rs.append(jnp.dot(xh, wa_ref[h], precision=HIGHEST, preferred_element_type=f32))
        gs.append(jnp.dot(xh, wx_ref[h], precision=HIGHEST, preferred_element_type=f32))
    r = _sigmoid(jnp.concatenate(rs, axis=1) + ba_ref[...])
    g = _sigmoid(jnp.concatenate(gs, axis=1) + bx_ref[...])
    log_a = -RG_C * r * _softplus(-lam_ref[...])
    a = jnp.exp(log_a)
    u = jnp.sqrt(-jnp.tanh(log_a) * (a * a + 1.0)) * (g * xc)
    rowid = lax.broadcasted_iota(i32, (tr, 1), 0)
    s = 1
    while s < tr:
        if reverse:
            keep = rowid < tr - s
            shift = tr - s
        else:
            keep = rowid >= s
            shift = s
        a_sh = jnp.where(keep, pltpu.roll(a, shift, 0), 1.0)
        u_sh = jnp.where(keep, pltpu.roll(u, shift, 0), 0.0)
        u = u + a * u_sh
        a = a * a_sh
        s *= 2
    hseq = a * carry[0:1, :] + u
    o_ref[0] = hseq
    last = 0 if reverse else tr - 1
    carry[0:1, :] = hseq[last:last + 1, :]


def rglru_scan(xc, wa, wx, ba, bx, lam, nctx_blk, reverse):
    b, tt, c = xc.shape
    tr = ROW_BLOCK
    nblk = tt // tr
    if reverse:
        def blk(bi, i):
            return (bi, jnp.where(i < nctx_blk, nctx_blk - 1 - i, nblk - 1 - (i - nctx_blk)), 0)
    else:
        def blk(bi, i):
            return (bi, i, 0)
    vec = pl.BlockSpec((1, c), lambda bi, i: (0, 0))
    wspec = pl.BlockSpec((RG_HEADS, RG_HEAD_DIM, RG_HEAD_DIM), lambda bi, i: (0, 0, 0))
    return pl.pallas_call(
        functools.partial(_rglru_kernel, reverse),
        grid=(b, nblk),
        in_specs=[pl.BlockSpec((1, tr, c), blk), wspec, wspec, vec, vec, vec],
        out_specs=pl.BlockSpec((1, tr, c), blk),
        out_shape=jax.ShapeDtypeStruct((b, tt, c), f32),
        scratch_shapes=[pltpu.VMEM((SUBLANES, c), f32)],
        compiler_params=_cparams(("arbitrary", "arbitrary")),
        name="rglru_rev" if reverse else "rglru_fwd",
    )(xc, wa, wx, ba.reshape(1, c), bx.reshape(1, c), lam.reshape(1, c))


def _dft_kernel(t_len, x_ref, o_ref, c2, s2, rot):
    mi = pl.program_id(0)
    ki = pl.program_id(1)
    bm, bk = c2.shape
    w0 = 2.0 * math.pi / t_len

    @pl.when(ki == 0)
    def _():
        k = mi * bm + lax.broadcasted_iota(i32, (bm, bk), 0)
        a = lax.broadcasted_iota(i32, (bm, bk), 1)
        ang = ((k * a) & (t_len - 1)).astype(f32) * w0
        c2[...] = jnp.cos(ang)
        s2[...] = jnp.sin(ang)
        kcol = mi * bm + lax.broadcasted_iota(i32, (bm, 1), 0)
        angd = ((kcol * bk) & (t_len - 1)).astype(f32) * w0
        rot[0] = jnp.ones((bm, 1), f32)
        rot[1] = jnp.zeros((bm, 1), f32)
        rot[2] = jnp.cos(angd)
        rot[3] = jnp.sin(angd)
        o_ref[...] = jnp.zeros_like(o_ref)

    c1, s1, cd, sd = rot[0], rot[1], rot[2], rot[3]
    rot[0] = c1 * cd - s1 * sd
    rot[1] = s1 * cd + c1 * sd
    cm = (c1 * c2[...] - s1 * s2[...]).astype(bf16)
    sm = (s1 * c2[...] + c1 * s2[...]).astype(bf16)
    w = x_ref.shape[2]
    for b in range(x_ref.shape[0]):
        xb = x_ref[b].astype(bf16)
        o_ref[b, :, 0:w] += jnp.dot(cm, xb, preferred_element_type=f32)
        o_ref[b, :, w:2 * w] += jnp.dot(sm, xb, preferred_element_type=f32)


def dft_positions(p, col_blk, width, row0, t_len):
    b = p.shape[0]
    assert t_len & (t_len - 1) == 0
    bk = min(ROW_BLOCK, t_len)
    bm = min(1024, t_len)
    r0 = row0 // bk
    return pl.pallas_call(
        functools.partial(_dft_kernel, t_len),
        grid=(t_len // bm, t_len // bk),
        in_specs=[pl.BlockSpec((b, bk, width), lambda mi, ki: (0, r0 + ki, col_blk))],
        out_specs=pl.BlockSpec((b, bm, 2 * width), lambda mi, ki: (0, mi, 0)),
        out_shape=jax.ShapeDtypeStruct((b, t_len, 2 * width), f32),
        scratch_shapes=[pltpu.VMEM((bm, bk), f32), pltpu.VMEM((bm, bk), f32), pltpu.VMEM((4, bm, 1), f32)],
        compiler_params=_cparams(("arbitrary", "arbitrary")),
        name="dft_positions",
    )(p)


def _residual_and_ffn_norm(h, y, mod_ref, gf_ref, hout_ref, f_ref):
    hn = h + mod_ref[0, 2:3, :] * y
    hout_ref[0] = hn
    f_ref[0] = _rms(hn, gf_ref[...]) * (1.0 + mod_ref[0, 4:5, :]) + mod_ref[0, 3:4, :]


def _merge_even_kernel(nctx_blk, sc_ctx, sc_lat, hf_ref, hb_ref, gate_ref, z_ref, h_ref, mod_ref, cc_ref, cs_ref,
                       fw_ref, fb_ref, wa_ref, wb_ref, gf_ref, hout_ref, f_ref):
    i = pl.program_id(1)
    ya = (hf_ref[0] + hb_ref[0]) * _gelu_tanh(gate_ref[0])
    z = z_ref[0]
    w = z.shape[1] // 2
    zz = (jnp.dot(z[:, :w], cc_ref[...], precision=HIGHEST, preferred_element_type=f32)
          - jnp.dot(z[:, w:], cs_ref[...], precision=HIGHEST, preferred_element_type=f32))
    zz = zz * jnp.where(i < nctx_blk, sc_ctx, sc_lat)
    yb = jnp.dot(zz, fw_ref[...], precision=HIGHEST, preferred_element_type=f32) + fb_ref[...]
    y = (jnp.dot(ya.astype(bf16), wa_ref[...], preferred_element_type=f32)
         + jnp.dot(yb.astype(bf16), wb_ref[...], preferred_element_type=f32))
    _residual_and_ffn_norm(h_ref[0], y, mod_ref, gf_ref, hout_ref, f_ref)


def merge_even(hf, hb, p, z, h, mod, fw_bd, fb, w_out, gf, nctx_blk, t_ctx, t_lat):
    b, tt, d = h.shape
    tr = ROW_BLOCK
    jj, kk = np.meshgrid(np.arange(FN_GROUP_DIM), np.arange(FN_GROUP_DIM), indexing="ij")
    ang = 2.0 * np.pi * jj * kk / FN_GROUP_DIM
    eye = np.eye(FN_GROUPS)
    cc = jnp.asarray(np.kron(eye, np.cos(ang)), f32)
    cs = jnp.asarray(np.kron(eye, np.sin(ang)), f32)
    sc_ctx = 1.0 / math.sqrt(t_ctx * FN_GROUP_DIM)
    sc_lat = 1.0 / math.sqrt(t_lat * FN_GROUP_DIM)

    def rows(wd, cb=0):
        return pl.BlockSpec((1, tr, wd), lambda bi, i: (bi, i, cb))

    def full(shape):
        return pl.BlockSpec(shape, lambda bi, i: (0,) * len(shape))

    return pl.pallas_call(
        functools.partial(_merge_even_kernel, nctx_blk, sc_ctx, sc_lat),
        grid=(b, tt // tr),
        in_specs=[
            rows(RG_WIDTH), rows(RG_WIDTH), rows(RG_WIDTH, 1), rows(2 * FN_WIDTH), rows(d),
            pl.BlockSpec((1, SUBLANES, d), lambda bi, i: (_mod_row(bi, i, nctx_blk), 0, 0)),
            full((FN_WIDTH, FN_WIDTH)), full((FN_WIDTH, FN_WIDTH)), full((FN_WIDTH, FN_WIDTH)), full((1, FN_WIDTH)),
            full((RG_WIDTH, d)), full((FN_WIDTH, d)), full((1, d)),
        ],
        out_specs=[rows(d), rows(d)],
        out_shape=[jax.ShapeDtypeStruct((b, tt, d), f32)] * 2,
        compiler_params=_cparams(("arbitrary", "arbitrary")),
        name="merge_even",
    )(hf, hb, p, z, h, mod, cc, cs, fw_bd, fb.reshape(1, FN_WIDTH),
      w_out[:RG_WIDTH].astype(bf16), w_out[RG_WIDTH:].astype(bf16), gf.reshape(1, d))


def _merge_odd_kernel(hs0_ref, hs1_ref, o_ref, at_ref, h_ref, mod_ref, hg_ref, wm_ref, wa_ref, gf_ref, hout_ref, f_ref):
    hs = hs0_ref[0, 0] + hs1_ref[0, 0]
    parts = []
    for hd in range(ML_HEADS):
        sl = slice(hd * ML_HEAD_DIM, (hd + 1) * ML_HEAD_DIM)
        parts.append(_rms(hs[:, sl], hg_ref[:, sl]))
    yml = _sigmoid(o_ref[0]) * jnp.concatenate(parts, axis=1)
    y = (jnp.dot(yml.astype(bf16), wm_ref[...], preferred_element_type=f32)
         + jnp.dot(at_ref[0].astype(bf16), wa_ref[...], preferred_element_type=f32))
    _residual_and_ffn_norm(h_ref[0], y, mod_ref, gf_ref, hout_ref, f_ref)


def merge_odd(hs, p, attn, h, mod, head_g, w_ml, w_at, gf, nctx_blk):
    b, t, ap = attn.shape
    d = h.shape[2]
    tr = ROW_BLOCK

    def rows(wd, cb=0, off=0):
        return pl.BlockSpec((1, tr, wd), lambda bi, i: (bi, i + off, cb))

    def full(shape):
        return pl.BlockSpec(shape, lambda bi, i: (0,) * len(shape))

    return pl.pallas_call(
        _merge_odd_kernel,
        grid=(b, t // tr),
        in_specs=[
            pl.BlockSpec((1, 1, tr, ML_WIDTH), lambda bi, i: (0, bi, i + nctx_blk, 0)),
            pl.BlockSpec((1, 1, tr, ML_WIDTH), lambda bi, i: (1, bi, i + nctx_blk, 0)),
            rows(ML_WIDTH, OD_O // ML_WIDTH, nctx_blk), rows(ap), rows(d, 0, nctx_blk),
            pl.BlockSpec((1, SUBLANES, d), lambda bi, i: (bi, 0, 0)),
            full((1, ML_WIDTH)), full((ML_WIDTH, d)), full((ap, d)), full((1, d)),
        ],
        out_specs=[rows(d), rows(d)],
        out_shape=[jax.ShapeDtypeStruct((b, t, d), f32)] * 2,
        compiler_params=_cparams(("arbitrary", "arbitrary")),
        name="merge_odd",
    )(hs, hs, p, attn, h, mod, head_g.reshape(1, ML_WIDTH), w_ml, w_at, gf.reshape(1, d))


def _mlstm_kernel(qk_ref, v_ref, g_ref, gt_ref, gb_ref, gbt_ref, o_ref, ct, nst, mst):
    d = pl.program_id(0)
    c = pl.program_id(2)
    L = ML_CHUNK

    @pl.when(c == 0)
    def _():
        ct[...] = jnp.zeros_like(ct)
        nst[...] = jnp.zeros_like(nst)
        mst[...] = jnp.zeros_like(mst)

    ri = lax.broadcasted_iota(i32, (L, L), 0)
    ci = lax.broadcasted_iota(i32, (L, L), 1)
    sgn = 1 - 2 * d
    tri = (ci - ri) * sgn <= 0
    trit = (ri - ci) * sgn <= 0
    gcol = g_ref[0] + gb_ref[...]
    grow = gt_ref[0] + gbt_ref[...]
    lane = lax.broadcasted_iota(i32, (1, LANES), 1)
    for hd in range(ML_HEADS):
        sl = slice(hd * ML_HEAD_DIM, (hd + 1) * ML_HEAD_DIM)
        q = qk_ref[0, :, sl] * (ML_HEAD_DIM ** -0.5)
        k = qk_ref[0, :, ML_WIDTH + hd * ML_HEAD_DIM:ML_WIDTH + (hd + 1) * ML_HEAD_DIM]
        v = v_ref[0, :, sl]
        li = d * 8 + hd
        ig_col = jnp.sum(jnp.where(lane == li, gcol, 0.0), axis=1, keepdims=True)
        fg_col = jnp.sum(jnp.where(lane == li + 4, gcol, 0.0), axis=1, keepdims=True)
        rsel = lax.broadcasted_iota(i32, (ML_N_GATES, 1), 0)
        ig_row = jnp.sum(jnp.where(rsel == li, grow, 0.0), axis=0, keepdims=True)
        fg_row = jnp.sum(jnp.where(rsel == li + 4, grow, 0.0), axis=0, keepdims=True)
        lf_col = -_softplus(-fg_col)
        lf_row = -_softplus(-fg_row)
        b_col = jnp.sum(jnp.where(tri, lf_row, 0.0), axis=1, keepdims=True)
        b_row = jnp.sum(jnp.where(trit, lf_col, 0.0), axis=0, keepdims=True)
        m_prev = mst[hd, 0:1, 0:1]
        n_prev = nst[hd, 0:1, :]
        ct_prev = ct[hd]
        dm = jnp.where(tri, b_col - b_row + ig_row, -jnp.inf)
        inter = b_col + m_prev
        m_row = jnp.maximum(jnp.max(dm, axis=1, keepdims=True), inter)
        qb, kb, vb = q.astype(bf16), k.astype(bf16), v.astype(bf16)
        s = lax.dot_general(qb, kb, (((1,), (1,)), ((), ())), preferred_element_type=f32) * jnp.exp(dm - m_row)
        w_inter = jnp.exp(inter - m_row)
        num = (jnp.dot(s.astype(bf16), vb, preferred_element_type=f32)
               + w_inter * jnp.dot(qb, ct_prev.astype(bf16), preferred_element_type=f32))
        den = jnp.sum(s, axis=1, keepdims=True) + w_inter * jnp.sum(q * n_prev, axis=1, keepdims=True)
        o_ref[0, 0, :, sl] = num / jnp.maximum(jnp.abs(den), jnp.exp(-m_row))
        b_l = jnp.sum(lf_col, axis=0, keepdims=True)
        log_wk = b_l - b_col + ig_col
        m_new = jnp.maximum(b_l + m_prev, jnp.max(log_wk, axis=0, keepdims=True))
        kw = jnp.exp(log_wk - m_new) * k
        decay = jnp.exp(b_l + m_prev - m_new)
        ct[hd] = decay * ct_prev + lax.dot_general(kw.astype(bf16), vb, (((0,), (0,)), ((), ())),
                                                   preferred_element_type=f32)
        nst[hd, 0:1, :] = decay * n_prev + jnp.sum(kw, axis=0, keepdims=True)
        mst[hd] = jnp.broadcast_to(m_new, (SUBLANES, LANES))


def mlstm(qkc, p, gt, gate_b, nctx_chunks):
    b, tt, _ = qkc.shape
    L = ML_CHUNK
    nch = tt // L

    def chunk(d, c):
        return jnp.where(d == 0, c, jnp.where(c < nctx_chunks, nctx_chunks - 1 - c, nch - 1 - (c - nctx_chunks)))

    gb = jnp.zeros((1, LANES), f32).at[0, :ML_N_GATES].set(gate_b)
    return pl.pallas_call(
        _mlstm_kernel,
        grid=(2, b, nch),
        in_specs=[
            pl.BlockSpec((1, L, 2 * ML_WIDTH), lambda d, bi, c: (bi, chunk(d, c), 0)),
            pl.BlockSpec((1, L, ML_WIDTH), lambda d, bi, c: (bi, chunk(d, c), OD_V // ML_WIDTH)),
            pl.BlockSpec((1, L, LANES), lambda d, bi, c: (bi, chunk(d, c), OD_G // LANES)),
            pl.BlockSpec((1, ML_N_GATES, L), lambda d, bi, c: (bi, 0, chunk(d, c))),
            pl.BlockSpec((1, LANES), lambda d, bi, c: (0, 0)),
            pl.BlockSpec((ML_N_GATES, 1), lambda d, bi, c: (0, 0)),
        ],
        out_specs=pl.BlockSpec((1, 1, L, ML_WIDTH), lambda d, bi, c: (d, bi, chunk(d, c), 0)),
        out_shape=jax.ShapeDtypeStruct((2, b, tt, ML_WIDTH), f32),
        scratch_shapes=[
            pltpu.VMEM((ML_HEADS, ML_HEAD_DIM, ML_HEAD_DIM), f32),
            pltpu.VMEM((ML_HEADS, SUBLANES, LANES), f32),
            pltpu.VMEM((ML_HEADS, SUBLANES, LANES), f32),
        ],
        compiler_params=_cparams(("arbitrary", "arbitrary", "arbitrary")),
        name="mlstm",
    )(qkc, p, p, gt, gb, gate_b.reshape(ML_N_GATES, 1))


def _mla_prep_kernel(nctx_blk, p_ref, qg_ref, kg_ref, wq_ref, wqs_ref, wk_ref, wv_ref, em_ref, es_ref,
                     q_ref, k_ref, v_ref):
    i = pl.program_id(1)
    tr = p_ref.shape[1]
    cq = _rms(p_ref[0, :, OD_CQ:OD_CKV], qg_ref[...]).astype(bf16)
    ckv = _rms(p_ref[0, :, OD_CKV:OD_KR], kg_ref[...]).astype(bf16)
    kr = p_ref[0, :, OD_KR:OD_N]
    lane = lax.broadcasted_iota(i32, (1, MLA_HEAD_PAD), 1)
    half = MLA_ROPE // 2
    pair = jnp.where(lane < MLA_NOPE + half, lane - MLA_NOPE, lane - MLA_NOPE - half)
    is_rope = jnp.logical_and(lane >= MLA_NOPE, lane < MLA_NOPE + MLA_ROPE)
    use_row = pair < half // 2
    fidx = jnp.clip(jnp.where(use_row, pair, pair - half // 2), 0, half // 2 - 1).astype(f32)
    inv = jnp.exp(fidx * (-math.log(ROPE_THETA) / (half // 2)))
    t = (i - nctx_blk) * tr + lax.broadcasted_iota(i32, (tr, 1), 0)
    pos = jnp.where(use_row, (t >> GRID_SHIFT).astype(f32), (t & (GRID_W - 1)).astype(f32))
    ang = pos * inv
    rot = jnp.logical_and(is_rope, i >= nctx_blk)
    cos1 = jnp.where(rot, jnp.cos(ang), 1.0)
    sin1 = jnp.where(rot, jnp.sin(ang), 0.0)
    cos_t = jnp.concatenate([cos1] * MLA_HEADS, axis=1)
    sin_t = jnp.concatenate([sin1] * MLA_HEADS, axis=1)
    scale = (MLA_NOPE + MLA_ROPE) ** -0.5
    q = (jnp.dot(cq, wq_ref[...], preferred_element_type=f32) * cos_t
         + jnp.dot(cq, wqs_ref[...], preferred_element_type=f32) * sin_t)
    q_ref[0] = (q * scale).astype(bf16)
    k = (jnp.dot(ckv, wk_ref[...], preferred_element_type=f32)
         + jnp.dot(kr, em_ref[...], precision=HIGHEST, preferred_element_type=f32) * cos_t
         + jnp.dot(kr, es_ref[...], precision=HIGHEST, preferred_element_type=f32) * sin_t)
    k_ref[0] = k.astype(bf16)
    v_ref[0] = jnp.dot(ckv, wv_ref[...], preferred_element_type=f32).astype(bf16)


def _mla_weights(w_uq, w_ukv):
    hp = MLA_HEAD_PAD
    half = MLA_ROPE // 2
    wq = w_uq.reshape(MLA_Q_LORA, MLA_HEADS, MLA_NOPE + MLA_ROPE)
    nope, x1, x2 = wq[..., :MLA_NOPE], wq[..., MLA_NOPE::2], wq[..., MLA_NOPE + 1::2]
    zq = jnp.zeros((MLA_Q_LORA, MLA_HEADS, hp - MLA_NOPE - MLA_ROPE), f32)
    wq_main = jnp.concatenate([nope, x1, x2, zq], axis=-1).reshape(MLA_Q_LORA, MLA_HEADS * hp)
    wq_swap = jnp.concatenate([jnp.zeros_like(nope), -x2, x1, zq], axis=-1).reshape(MLA_Q_LORA, MLA_HEADS * hp)
    wkv = w_ukv.reshape(MLA_KV_LORA, MLA_HEADS, MLA_NOPE + MLA_V)
    zk = jnp.zeros((MLA_KV_LORA, MLA_HEADS, hp - MLA_NOPE), f32)
    wk = jnp.concatenate([wkv[..., :MLA_NOPE], zk], axis=-1).reshape(MLA_KV_LORA, MLA_HEADS * hp)
    wv = jnp.concatenate([wkv[..., MLA_NOPE:], jnp.zeros((MLA_KV_LORA, MLA_HEADS, hp - MLA_V), f32)],
                         axis=-1).reshape(MLA_KV_LORA, MLA_HEADS * hp)
    em = np.zeros((LANES, MLA_HEADS, hp), np.float32)
    es = np.zeros((LANES, MLA_HEADS, hp), np.float32)
    for j in range(half):
        em[2 * j, :, MLA_NOPE + j] = 1.0
        em[2 * j + 1, :, MLA_NOPE + half + j] = 1.0
        es[2 * j + 1, :, MLA_NOPE + j] = -1.0
        es[2 * j, :, MLA_NOPE + half + j] = 1.0
    em = jnp.asarray(em.reshape(LANES, MLA_HEADS * hp))
    es = jnp.asarray(es.reshape(LANES, MLA_HEADS * hp))
    return wq_main.astype(bf16), wq_swap.astype(bf16), wk.astype(bf16), wv.astype(bf16), em, es


def mla_prep(p, q_norm_g, kv_norm_g, w_uq, w_ukv, nctx_blk):
    b, tt, n = p.shape
    tr = ROW_BLOCK
    hw = MLA_HEADS * MLA_HEAD_PAD
    wq, wqs, wk, wv, em, es = _mla_weights(w_uq, w_ukv)

    def full(shape):
        return pl.BlockSpec(shape, lambda bi, i: (0,) * len(shape))

    out = pl.BlockSpec((1, tr, hw), lambda bi, i: (bi, i, 0))
    return pl.pallas_call(
        functools.partial(_mla_prep_kernel, nctx_blk),
        grid=(b, tt // tr),
        in_specs=[pl.BlockSpec((1, tr, n), lambda bi, i: (bi, i, 0)),
                  full((1, MLA_Q_LORA)), full((1, MLA_KV_LORA)),
                  full((MLA_Q_LORA, hw)), full((MLA_Q_LORA, hw)), full((MLA_KV_LORA, hw)), full((MLA_KV_LORA, hw)),
                  full((LANES, hw)), full((LANES, hw))],
        out_specs=[out, out, out],
        out_shape=[jax.ShapeDtypeStruct((b, tt, hw), bf16)] * 3,
        compiler_params=_cparams(("arbitrary", "arbitrary")),
        name="mla_prep",
    )(p, q_norm_g.reshape(1, MLA_Q_LORA), kv_norm_g.reshape(1, MLA_KV_LORA), wq, wqs, wk, wv, em, es)


def _mla_attn_kernel(q_ref, k_ref, v_ref, o_ref):
    tq = q_ref.shape[1]
    cq = tq // ATTN_Q_CHUNKS
    for c in range(ATTN_Q_CHUNKS):
        rows = slice(c * cq, (c + 1) * cq)
        s = lax.dot_general(q_ref[0, rows, :], k_ref[0], (((1,), (1,)), ((), ())), preferred_element_type=f32)
        m = jnp.max(s, axis=-1, keepdims=True)
        e = jnp.exp(s - m)
        l = jnp.sum(e, axis=-1, keepdims=True)
        o_ref[0, rows, :] = jnp.dot(e.astype(bf16), v_ref[0], preferred_element_type=f32) / l


def mla_attention(q, k, v, nctx_blk):
    b, tt, hw = q.shape
    tq = ROW_BLOCK
    nq = tt // tq - nctx_blk
    hp = MLA_HEAD_PAD
    kv = pl.BlockSpec((1, tt, hp), lambda bi, h, qi: (bi, 0, h))
    return pl.pallas_call(
        _mla_attn_kernel,
        grid=(b, MLA_HEADS, nq),
        in_specs=[pl.BlockSpec((1, tq, hp), lambda bi, h, qi: (bi, qi + nctx_blk, h)), kv, kv],
        out_specs=pl.BlockSpec((1, tq, hp), lambda bi, h, qi: (bi, qi, h)),
        out_shape=jax.ShapeDtypeStruct((b, nq * tq, hw), f32),
        compiler_params=_cparams(("arbitrary", "arbitrary", "arbitrary")),
        name="mla_attention",
    )(q, k, v)


def _topk_rows(s, k, payload=None):
    n = s.shape[0]
    rid = lax.broadcasted_iota(i32, s.shape, 0)
    vals, outs = [], []
    for _ in range(k):
        m = jnp.max(s, axis=0, keepdims=True)
        am = jnp.min(jnp.where(s == m, rid, n), axis=0, keepdims=True)
        hit = rid == am
        vals.append(m)
        outs.append(am if payload is None else jnp.sum(jnp.where(hit, payload, 0), axis=0, keepdims=True))
        s = jnp.where(hit, -jnp.inf, s)
    return jnp.concatenate(vals, axis=0), jnp.concatenate(outs, axis=0)


def _pruned_candidates(vals, idxs):
    k = PEER_TOPK
    hk = k // 2
    bcol = lax.broadcasted_iota(i32, (hk, 1), 0)
    cand = [vals[0][0:1, :] + vals[1]]
    cidx = [idxs[0][0:1, :] * PEER_N_KEYS + idxs[1]]
    for a in range(1, hk):
        ok = (a + 1) * (bcol + 1) <= k
        cand.append(jnp.where(ok, vals[0][a:a + 1, :] + vals[1][0:hk, :], -jnp.inf))
        cidx.append(idxs[0][a:a + 1, :] * PEER_N_KEYS + idxs[1][0:hk, :])
    cand.append(vals[0][hk:k, :] + vals[1][0:1, :])
    cidx.append(idxs[0][hk:k, :] * PEER_N_KEYS + idxs[1][0:1, :])
    return jnp.concatenate(cand, axis=0), jnp.concatenate(cidx, axis=0)


def _peer_route_kernel(f_ref, wq_ref, keys_ref, code_ref, g_ref):
    q = jnp.dot(f_ref[...].astype(bf16), wq_ref[...], preferred_element_type=f32)
    hk = PEER_KEY_DIM // 2
    for h in range(PEER_HEADS):
        vals, idxs = [], []
        for p in range(2):
            c0 = (h * 2 + p) * hk
            qhp = q[:, c0:c0 + hk].astype(bf16)
            st = lax.dot_general(keys_ref[p, h], qhp, (((1,), (1,)), ((), ())), preferred_element_type=f32)
            v, ix = _topk_rows(st, PEER_TOPK)
            vals.append(v)
            idxs.append(ix)
        cand, cidx = _pruned_candidates(vals, idxs)
        top_s, e = _topk_rows(cand, PEER_TOPK, payload=cidx)
        ex = jnp.exp(top_s - top_s[0:1, :])
        rows = slice(h * PEER_TOPK, (h + 1) * PEER_TOPK)
        g_ref[rows, :] = ex / jnp.sum(ex, axis=0, keepdims=True)
        code_ref[rows, :] = (e >> 1) * SUBLANES + ((1 - (e & 1)) << LOW_HALF_BIT)


def peer_route(f2, wq, keys):
    r, d = f2.shape
    tb = PEER_TB
    out = pl.BlockSpec((PEER_PICKS, tb), lambda i: (0, i))
    return pl.pallas_call(
        _peer_route_kernel,
        grid=(r // tb,),
        in_specs=[
            pl.BlockSpec((tb, d), lambda i: (i, 0)),
            pl.BlockSpec(wq.shape, lambda i: (0, 0)),
            pl.BlockSpec(keys.shape, lambda i: (0, 0, 0, 0)),
        ],
        out_specs=[out, out],
        out_shape=[jax.ShapeDtypeStruct((PEER_PICKS, r), i32), jax.ShapeDtypeStruct((PEER_PICKS, r), f32)],
        compiler_params=_cparams(("arbitrary",)),
        name="peer_route",
    )(f2, wq, keys)


def pack_expert_table(t):
    e, d = t.shape
    bits = lax.bitcast_convert_type(t.astype(bf16), jnp.uint16).astype(jnp.uint32)
    bits = bits.reshape(e // 2, 2, SUBLANES, d // SUBLANES)
    word = bits[:, 0] | (bits[:, 1] << 16)
    return lax.bitcast_convert_type(word, i32).reshape(e // 2 * SUBLANES, d // SUBLANES)


def _load_table_once(tab_hbm, tab_vmem, sem):
    @pl.when(pl.program_id(0) == 0)
    def _():
        cp = pltpu.make_async_copy(tab_hbm, tab_vmem, sem)
        cp.start()
        cp.wait()


def _slot_major(a, tb):
    r = a.shape[0]
    ng = PEER_PICKS // SUBLANES
    return a.reshape(r // tb, tb, ng, SUBLANES).transpose(0, 3, 1, 2).reshape(r // tb, SUBLANES, tb * ng)


def _fetch_block(src_hbm, dst_smems, sems):
    i = pl.program_id(0)
    n = dst_smems[0].shape[0] // 2
    half = i % 2

    def copies(blk, h):
        off = pl.multiple_of(h * n, n)
        return [pltpu.make_async_copy(src_hbm.at[blk, s], dst_smems[s].at[pl.ds(off, n)], sems.at[h, s])
                for s in range(SUBLANES)]

    @pl.when(i == 0)
    def _():
        for cp in copies(0, 0):
            cp.start()

    for cp in copies(i, half):
        cp.wait()

    @pl.when(i + 1 < pl.num_programs(0))
    def _():
        for cp in copies(i + 1, 1 - half):
            cp.start()

    return half * n


def _rotate_tiles(tb, mbufs, row_s, row_base, tab_vmem, consume):
    nb = len(mbufs)
    ng = PEER_PICKS // SUBLANES

    def copy_tiles(t, slot):
        for jg in range(ng):
            g = row_base + t * ng + jg
            for s in range(SUBLANES):
                row = pl.multiple_of(row_s[s][g], SUBLANES)
                mbufs[slot][pl.ds((jg * SUBLANES + s) * SUBLANES, SUBLANES), :] = tab_vmem[pl.ds(row, SUBLANES), :]

    def rotate(p, c):
        for q in range(nb):
            consume(p * nb + q, mbufs[q])
            copy_tiles((p + 1) * nb + q, q)
        return c

    for q in range(nb):
        copy_tiles(q, q)
    lax.fori_loop(0, tb // nb - 1, rotate, 0)
    for q in range(nb):
        consume(tb - nb + q, mbufs[q])


def _peer_act_kernel(row_hbm, code_ref, x_ref, g_ref, tab_hbm, ex_ref, rd_ref, wx_ref, tab_vmem, abuf, sem_tab, sem_a,
                     *bufs):
    i = pl.program_id(0)
    tb = x_ref.shape[0]
    nr = 2 * SUBLANES
    nx = PEER_PICKS * nr
    nb = PEER_TILE_DEPTH
    mbufs, row_s = bufs[:nb], bufs[nb:]
    _load_table_once(tab_hbm, tab_vmem, sem_tab)
    row_base = _fetch_block(row_hbm, row_s, sem_a)
    lane = lax.broadcasted_iota(i32, (nr, nx), 1)
    sub = lax.broadcasted_iota(i32, (nr, nx), 0)
    own = (lane & (nr - 1)) == sub

    def dots(t, mbuf):
        gm = lax.dot_general(x_ref[t], pltpu.bitcast(mbuf[...], bf16), (((1,), (1,)), ((), ())),
                             preferred_element_type=f32)
        abuf[pl.ds(t, 1), :] = jnp.sum(jnp.where(own, gm, 0.0), axis=0, keepdims=True)

    _rotate_tiles(tb, mbufs, row_s, row_base, tab_vmem, dots)
    a = abuf[...]
    a_hi = a.astype(bf16)
    a_lo = (a - a_hi.astype(f32)).astype(bf16)
    low = ((code_ref[...] >> LOW_HALF_BIT) & 1) == 1
    act = jnp.where(
        low,
        jnp.dot(a_hi, rd_ref[0], preferred_element_type=f32) + jnp.dot(a_lo, rd_ref[0], preferred_element_type=f32),
        jnp.dot(a_hi, rd_ref[1], preferred_element_type=f32) + jnp.dot(a_lo, rd_ref[1], preferred_element_type=f32))
    w = g_ref[...] * _gelu_tanh(act)
    wx_ref[...] = (jnp.dot(jnp.where(low, w, 0.0).astype(bf16), ex_ref[0], preferred_element_type=f32)
                   + jnp.dot(jnp.where(low, 0.0, w).astype(bf16), ex_ref[1], preferred_element_type=f32))


def _expansion_matrices():
    ex = np.zeros((2, PEER_PICKS, PEER_PICKS * 2 * SUBLANES), np.float32)
    for j in range(PEER_PICKS):
        for s in range(SUBLANES):
            ex[0, j, j * 2 * SUBLANES + 2 * s] = 1.0
            ex[1, j, j * 2 * SUBLANES + 2 * s + 1] = 1.0
    return ex


def peer_act(row_sm, code, x2, g, tab):
    r = x2.shape[0]
    tb = PEER_TB
    ng = PEER_PICKS // SUBLANES
    nx = PEER_PICKS * 2 * SUBLANES
    ex = _expansion_matrices()
    anyspec = pl.BlockSpec(memory_space=pl.ANY)
    rows = pl.BlockSpec((tb, PEER_PICKS), lambda i: (i, 0))
    return pl.pallas_call(
        _peer_act_kernel,
        grid=(r // tb,),
        in_specs=[anyspec, rows, pl.BlockSpec((tb, 2 * SUBLANES, LANES), lambda i: (i, 0, 0)), rows, anyspec,
                  pl.BlockSpec((2, PEER_PICKS, nx), lambda i: (0, 0, 0)),
                  pl.BlockSpec((2, nx, PEER_PICKS), lambda i: (0, 0, 0))],
        out_specs=pl.BlockSpec((tb, nx), lambda i: (i, 0)),
        out_shape=jax.ShapeDtypeStruct((r, nx), f32),
        scratch_shapes=[
            pltpu.VMEM(tab.shape, i32),
            pltpu.VMEM((tb, nx), f32),
            pltpu.SemaphoreType.DMA, pltpu.SemaphoreType.DMA((2, SUBLANES)),
        ] + [pltpu.VMEM((PEER_PICKS * SUBLANES, LANES), i32)] * PEER_TILE_DEPTH
          + [pltpu.SMEM((2 * tb * ng,), i32)] * SUBLANES,
        compiler_params=_cparams(("arbitrary",), VMEM_LIMIT_TABLE),
        name="peer_act",
    )(row_sm, code, x2, g, tab, jnp.asarray(ex, bf16), jnp.asarray(ex.transpose(0, 2, 1), bf16))


def _peer_out_kernel(row_hbm, wx_ref, tab_hbm, o_ref, tab_vmem, sem_tab, sem_a, *bufs):
    i = pl.program_id(0)
    tb = o_ref.shape[0]
    nx = wx_ref.shape[1]
    nb = PEER_TILE_DEPTH
    mbufs, row_s = bufs[:nb], bufs[nb:]
    _load_table_once(tab_hbm, tab_vmem, sem_tab)
    row_base = _fetch_block(row_hbm, row_s, sem_a)
    lane = lax.broadcasted_iota(i32, (SUBLANES, nx), 1)
    sub = lax.broadcasted_iota(i32, (SUBLANES, nx), 0)
    diag = ((lane & (2 * SUBLANES - 1)) >> 1) == sub

    def contract(t, mbuf):
        wsel = jnp.where(diag, jnp.broadcast_to(wx_ref[pl.ds(t, 1), :], (SUBLANES, nx)), 0.0).astype(bf16)
        o_ref[t] = jnp.dot(wsel, pltpu.bitcast(mbuf[...], bf16), preferred_element_type=f32)

    _rotate_tiles(tb, mbufs, row_s, row_base, tab_vmem, contract)


def peer_out(row_sm, wx, tab):
    nblk, _, n = row_sm.shape
    ng = PEER_PICKS // SUBLANES
    tb = n // ng
    nx = wx.shape[1]
    assert tb % PEER_TILE_DEPTH == 0
    anyspec = pl.BlockSpec(memory_space=pl.ANY)
    return pl.pallas_call(
        _peer_out_kernel,
        grid=(nblk,),
        in_specs=[anyspec, pl.BlockSpec((tb, nx), lambda i: (i, 0)), anyspec],
        out_specs=pl.BlockSpec((tb, SUBLANES, LANES), lambda i: (i, 0, 0)),
        out_shape=jax.ShapeDtypeStruct((nblk * tb, SUBLANES, LANES), f32),
        scratch_shapes=[
            pltpu.VMEM(tab.shape, i32),
            pltpu.SemaphoreType.DMA, pltpu.SemaphoreType.DMA((2, SUBLANES)),
        ] + [pltpu.VMEM((PEER_PICKS * SUBLANES, LANES), i32)] * PEER_TILE_DEPTH
          + [pltpu.SMEM((2 * n,), i32)] * SUBLANES,
        compiler_params=_cparams(("arbitrary",), VMEM_LIMIT_TABLE),
        name="peer_out",
    )(row_sm, wx, tab)


def peer_ffn(f, wq, keys, u, v):
    b, r, d = f.shape
    f2 = f.reshape(b * r, d)
    codet, gt = peer_route(f2, wq.astype(bf16), keys.astype(bf16))
    code = codet.T
    row_sm = _slot_major(code & ROW_MASK, PEER_TB)
    x2 = jnp.repeat(f2.reshape(b * r, SUBLANES, d // SUBLANES), 2, axis=1).astype(bf16)
    wx = peer_act(row_sm, code, x2, gt.T, pack_expert_table(u))
    o = peer_out(row_sm, wx, pack_expert_table(v))
    return o.reshape(b, r, d)


def _final_kernel(h_ref, peer_ref, mod_ref, g_ref, o_ref):
    h = h_ref[0] + mod_ref[0, 5:6, :] * peer_ref[0]
    o_ref[0] = _rms(h, g_ref[...])


def final_norm(h, peer, mod, g):
    b, t, d = h.shape
    tr = ROW_BLOCK
    row = pl.BlockSpec((1, tr, d), lambda bi, i: (bi, i, 0))
    return pl.pallas_call(
        _final_kernel,
        grid=(b, t // tr),
        in_specs=[row, row, pl.BlockSpec((1, SUBLANES, d), lambda bi, i: (bi, 0, 0)),
                  pl.BlockSpec((1, d), lambda bi, i: (0, 0))],
        out_specs=row,
        out_shape=jax.ShapeDtypeStruct((b, t, d), f32),
        compiler_params=_cparams(("arbitrary", "arbitrary")),
        name="final_norm",
    )(h, peer, mod, g.reshape(1, d))


def _block_diag(w):
    g, a, c = w.shape
    out = jnp.zeros((g * a, g * c), w.dtype)
    for k in range(g):
        out = out.at[k * a:(k + 1) * a, k * c:(k + 1) * c].set(w[k])
    return out


def kernel(x, c, ctx, c_ctx, mod_w, mod_b, norm_mix_g, norm_ffn_g, ev_w_in, ev_w_out, rg_conv_w, rg_conv_b, rg_wa, rg_ba, rg_wx, rg_bx, rg_lam, fn_w, fn_b, od_w_in, od_w_out, ml_conv_w, ml_conv_b, ml_gate_b, ml_head_g, mla_q_norm_g, mla_w_uq, mla_kv_norm_g, mla_w_ukv, peer_w_q, peer_keys, peer_u, peer_v, final_g):
    b, t, d = x.shape
    tc = ctx.shape[1]
    assert b == 2 and tc % ROW_BLOCK == 0 and t % ROW_BLOCK == 0 and mod_w.shape[0] == 2
    nctx_blk = tc // ROW_BLOCK

    cvec = jnp.zeros((SUBLANES, d), f32).at[0:b].set(c).at[b].set(c_ctx)
    mods = modulation(cvec, mod_w, mod_b)
    mods = mods.reshape(2, SUBLANES, 6, d)[:, :3]
    mods = jnp.pad(mods, ((0, 0), (0, 0), (0, 2), (0, 0)))

    h0 = jnp.concatenate([ctx, x], axis=1)

    p = inproj(h0, mods[0], norm_mix_g[0], ev_w_in[0].astype(bf16), nctx_blk)
    xc = dwconv(p, RG_WIDTH, rg_conv_w[0], rg_conv_b[0], nctx_blk, apply_silu=False)
    hf = rglru_scan(xc, rg_wa[0, 0], rg_wx[0, 0], rg_ba[0, 0], rg_bx[0, 0], rg_lam[0, 0], nctx_blk, reverse=False)
    hb = rglru_scan(xc, rg_wa[0, 1], rg_wx[0, 1], rg_ba[0, 1], rg_bx[0, 1], rg_lam[0, 1], nctx_blk, reverse=True)
    fcol = 2 * RG_WIDTH // FN_WIDTH
    z = jnp.concatenate([dft_positions(p, fcol, FN_WIDTH, 0, tc), dft_positions(p, fcol, FN_WIDTH, tc, t)], axis=1)
    h1, f1 = merge_even(hf, hb, p, z, h0, mods[0], _block_diag(fn_w[0]), fn_b[0], ev_w_out[0], norm_ffn_g[0],
                        nctx_blk, tc, t)
    peer1 = peer_ffn(f1, peer_w_q[0], peer_keys[0], peer_u[0], peer_v[0])

    w = od_w_in[0]
    zpad = lambda n: jnp.zeros((d, n), f32)
    w_od = jnp.concatenate([w[:, :OD_G + ML_N_GATES], zpad(OD_CQ - OD_G - ML_N_GATES),
                            w[:, 2064:2064 + MLA_Q_LORA + MLA_KV_LORA + MLA_ROPE],
                            zpad(OD_N - OD_KR - MLA_ROPE)], axis=1).astype(bf16)
    h2, p2 = inproj(h1, mods[1], norm_mix_g[1], w_od, nctx_blk, peer=peer1, mod_prev=mods[0])
    qkc = dwconv(p2, 2 * ML_WIDTH, ml_conv_w[0], ml_conv_b[0], nctx_blk, apply_silu=True)
    gt = jnp.swapaxes(p2[:, :, OD_G:OD_G + ML_N_GATES], 1, 2)
    hs = mlstm(qkc, p2, gt, ml_gate_b[0], tc // ML_CHUNK)
    q, k, v = mla_prep(p2, mla_q_norm_g[0], mla_kv_norm_g[0], mla_w_uq[0], mla_w_ukv[0], nctx_blk)
    attn = mla_attention(q, k, v, nctx_blk)
    wo = od_w_out[0]
    w_at = jnp.concatenate([wo[ML_WIDTH:].reshape(MLA_HEADS, MLA_V, d),
                            jnp.zeros((MLA_HEADS, MLA_HEAD_PAD - MLA_V, d), f32)], axis=1)
    w_at = w_at.reshape(MLA_HEADS * MLA_HEAD_PAD, d).astype(bf16)
    h3, f3 = merge_odd(hs, p2, attn, h2, mods[1, :b], ml_head_g[0], wo[:ML_WIDTH].astype(bf16), w_at,
                       norm_ffn_g[1], nctx_blk)
    peer3 = peer_ffn(f3, peer_w_q[1], peer_keys[1], peer_u[1], peer_v[1])
    return final_norm(h3, peer3, mods[1, :b], final_g)
```

```python
import functools
import math

import numpy as np
import jax
import jax.numpy as jnp
from jax import lax
from jax.experimental import pallas as pl
from jax.experimental.pallas import tpu as pltpu

f32 = jnp.float32
bf16 = jnp.bfloat16
i32 = jnp.int32
HIGHEST = lax.Precision.HIGHEST

SUBLANES = 8
LANES = 128
VMEM_LIMIT = 48 * 1024 * 1024
VMEM_LIMIT_TABLE = 56 * 1024 * 1024

EPS = 1e-6
GRID_W = 64
GRID_SHIFT = 6
ROW_BLOCK = 256
CONV_W = 4
RG_HEADS, RG_HEAD_DIM = 6, 128
RG_WIDTH = RG_HEADS * RG_HEAD_DIM
RG_C = 8.0
FN_GROUPS, FN_GROUP_DIM = 4, 64
FN_WIDTH = FN_GROUPS * FN_GROUP_DIM
ML_HEADS, ML_HEAD_DIM = 4, 128
ML_WIDTH = ML_HEADS * ML_HEAD_DIM
ML_CHUNK = 128
ML_N_GATES = 2 * 2 * ML_HEADS
MLA_HEADS, MLA_NOPE, MLA_ROPE, MLA_V = 8, 64, 32, 64
MLA_Q_LORA, MLA_KV_LORA = 384, 256
MLA_HEAD_PAD = 128
ATTN_Q_CHUNKS = 2
ROPE_THETA = 10000.0
PEER_HEADS, PEER_KEY_DIM, PEER_N_KEYS, PEER_TOPK = 8, 256, 128, 16
PEER_PICKS = PEER_HEADS * PEER_TOPK
PEER_TB = 128
PEER_TILE_DEPTH = 8
ROW_MASK = 0xFFF8
HI_MASK = -65536
LOW_HALF_BIT = 20

OD_V, OD_O, OD_G, OD_CQ, OD_CKV, OD_KR, OD_N = 1024, 1536, 2048, 2176, 2560, 2816, 2944


def _cparams(sem, vmem=VMEM_LIMIT):
    return pltpu.CompilerParams(dimension_semantics=sem, vmem_limit_bytes=vmem)


def _gelu_tanh(x):
    return 0.5 * x * (1.0 + jnp.tanh(0.7978845608028654 * (x + 0.044715 * x * x * x)))


def _sigmoid(x):
    return 1.0 / (1.0 + jnp.exp(-x))


def _softplus(x):
    return jnp.maximum(x, 0.0) + jnp.log1p(jnp.exp(-jnp.abs(x)))


def _rms(x, g):
    return x * lax.rsqrt(jnp.mean(x * x, axis=-1, keepdims=True) + EPS) * g


def _mod_row(b, i, nctx_blk):
    return jnp.where(i < nctx_blk, 2, b)


def _mod_kernel(c_ref, w_ref, b_ref, o_ref):
    c = c_ref[...]
    s = c * _sigmoid(c)
    o_ref[0] = jnp.dot(s, w_ref[0], precision=HIGHEST, preferred_element_type=f32) + b_ref[0]


def modulation(cvec, mod_w, mod_b):
    depth, d, n = mod_w.shape
    tn = 1536
    return pl.pallas_call(
        _mod_kernel,
        grid=(depth, n // tn),
        in_specs=[
            pl.BlockSpec((SUBLANES, d), lambda l, j: (0, 0)),
            pl.BlockSpec((1, d, tn), lambda l, j: (l, 0, j)),
            pl.BlockSpec((1, 1, tn), lambda l, j: (l, 0, j)),
        ],
        out_specs=pl.BlockSpec((1, SUBLANES, tn), lambda l, j: (l, 0, j)),
        out_shape=jax.ShapeDtypeStruct((depth, SUBLANES, n), f32),
        compiler_params=_cparams(("arbitrary", "arbitrary")),
        name="modulation",
    )(cvec, mod_w, mod_b.reshape(depth, 1, n))


def _inproj_kernel(has_res, *refs):
    if has_res:
        h_ref, peer_ref, modp_ref, mod_ref, g_ref, w_ref, hout_ref, p_ref = refs
        h = h_ref[0] + modp_ref[0, 5:6, :] * peer_ref[0]
        hout_ref[0] = h
    else:
        h_ref, mod_ref, g_ref, w_ref, p_ref = refs
        h = h_ref[0]
    u = _rms(h, g_ref[...]) * (1.0 + mod_ref[0, 1:2, :]) + mod_ref[0, 0:1, :]
    p_ref[0] = jnp.dot(u.astype(bf16), w_ref[...], preferred_element_type=f32)


def inproj(h, mod, g, w, nctx_blk, peer=None, mod_prev=None):
    b, tt, d = h.shape
    n = w.shape[1]
    tr = ROW_BLOCK
    has_res = peer is not None
    row = pl.BlockSpec((1, tr, d), lambda bi, i: (bi, i, 0))
    modspec = pl.BlockSpec((1, SUBLANES, d), lambda bi, i: (_mod_row(bi, i, nctx_blk), 0, 0))
    in_specs = [row] + ([row, modspec] if has_res else []) + [
        modspec,
        pl.BlockSpec((1, d), lambda bi, i: (0, 0)),
        pl.BlockSpec((d, n), lambda bi, i: (0, 0)),
    ]
    pspec = pl.BlockSpec((1, tr, n), lambda bi, i: (bi, i, 0))
    pshape = jax.ShapeDtypeStruct((b, tt, n), f32)
    args = [h] + ([peer, mod_prev] if has_res else []) + [mod, g.reshape(1, d), w]
    return pl.pallas_call(
        functools.partial(_inproj_kernel, has_res),
        grid=(b, tt // tr),
        in_specs=in_specs,
        out_specs=[row, pspec] if has_res else pspec,
        out_shape=[jax.ShapeDtypeStruct((b, tt, d), f32), pshape] if has_res else pshape,
        compiler_params=_cparams(("arbitrary", "arbitrary")),
        name="inproj",
    )(*args)


def _conv_kernel(apply_silu, nctx_blk, nblk, prev_ref, cur_ref, next_ref, w_ref, b_ref, o_ref, ext):
    i = pl.program_id(1)
    tr = cur_ref.shape[1]
    has_prev = jnp.logical_and(i != 0, i != nctx_blk)
    has_next = jnp.logical_and(i != nctx_blk - 1, i != nblk - 1)
    ext[0:SUBLANES] = jnp.where(has_prev, prev_ref[0], 0.0)
    ext[SUBLANES:SUBLANES + tr] = cur_ref[0]
    ext[SUBLANES + tr:2 * SUBLANES + tr] = jnp.where(has_next, next_ref[0], 0.0)
    lo = CONV_W // 2
    y = b_ref[...] + w_ref[0:1, :] * ext[pl.ds(SUBLANES - lo, tr), :]
    for j in range(1, CONV_W):
        y = y + w_ref[j:j + 1, :] * ext[pl.ds(SUBLANES - lo + j, tr), :]
    if apply_silu:
        y = y * _sigmoid(y)
    o_ref[0] = y


def dwconv(p, c, w, bias, nctx_blk, apply_silu):
    b, tt, _ = p.shape
    tr = ROW_BLOCK
    nblk = tt // tr
    hb = tr // SUBLANES
    return pl.pallas_call(
        functools.partial(_conv_kernel, apply_silu, nctx_blk, nblk),
        grid=(b, nblk),
        in_specs=[
            pl.BlockSpec((1, SUBLANES, c), lambda bi, i: (bi, jnp.maximum(i * hb - 1, 0), 0)),
            pl.BlockSpec((1, tr, c), lambda bi, i: (bi, i, 0)),
            pl.BlockSpec((1, SUBLANES, c), lambda bi, i: (bi, jnp.minimum((i + 1) * hb, nblk * hb - 1), 0)),
            pl.BlockSpec((CONV_W, c), lambda bi, i: (0, 0)),
            pl.BlockSpec((1, c), lambda bi, i: (0, 0)),
        ],
        out_specs=pl.BlockSpec((1, tr, c), lambda bi, i: (bi, i, 0)),
        out_shape=jax.ShapeDtypeStruct((b, tt, c), f32),
        scratch_shapes=[pltpu.VMEM((tr + 2 * SUBLANES, c), f32)],
        compiler_params=_cparams(("arbitrary", "arbitrary")),
        name="dwconv",
    )(p, p, p, w, bias.reshape(1, c))


def _rglru_kernel(reverse, xc_ref, wa_ref, wx_ref, ba_ref, bx_ref, lam_ref, o_ref, carry):
    i = pl.program_id(1)

    @pl.when(i == 0)
    def _():
        carry[...] = jnp.zeros_like(carry)

    xc = xc_ref[0]
    tr = xc.shape[0]
    rs, gs = [], []
    for h in range(RG_HEADS):
        xh = xc[:, h * RG_HEAD_DIM:(h + 1) * RG_HEAD_DIM]
        rs.append(jnp.dot(xh, wa_ref[h], precision=HIGHEST, preferred_element_type=f32))
        gs.append(jnp.dot(xh, wx_ref[h], precision=HIGHEST, preferred_element_type=f32))
    r = _sigmoid(jnp.concatenate(rs, axis=1) + ba_ref[...])
    g = _sigmoid(jnp.concatenate(gs, axis=1) + bx_ref[...])
    log_a = -RG_C * r * _softplus(-lam_ref[...])
    a = jnp.exp(log_a)
    u = jnp.sqrt(-jnp.tanh(log_a) * (a * a + 1.0)) * (g * xc)
    rowid = lax.broadcasted_iota(i32, (tr, 1), 0)
    s = 1
    while s < tr:
        if reverse:
            keep = rowid < tr - s
            shift = tr - s
        else:
            keep = rowid >= s
            shift = s
        a_sh = jnp.where(keep, pltpu.roll(a, shift, 0), 1.0)
        u_sh = jnp.where(keep, pltpu.roll(u, shift, 0), 0.0)
        u = u + a * u_sh
        a = a * a_sh
        s *= 2
    hseq = a * carry[0:1, :] + u
    o_ref[0] = hseq
    last = 0 if reverse else tr - 1
    carry[0:1, :] = hseq[last:last + 1, :]


def rglru_scan(xc, wa, wx, ba, bx, lam, nctx_blk, reverse):
    b, tt, c = xc.shape
    tr = ROW_BLOCK
    nblk = tt // tr
    if reverse:
        def blk(bi, i):
            return (bi, jnp.where(i < nctx_blk, nctx_blk - 1 - i, nblk - 1 - (i - nctx_blk)), 0)
    else:
        def blk(bi, i):
            return (bi, i, 0)
    vec = pl.BlockSpec((1, c), lambda bi, i: (0, 0))
    wspec = pl.BlockSpec((RG_HEADS, RG_HEAD_DIM, RG_HEAD_DIM), lambda bi, i: (0, 0, 0))
    return pl.pallas_call(
        functools.partial(_rglru_kernel, reverse),
        grid=(b, nblk),
        in_specs=[pl.BlockSpec((1, tr, c), blk), wspec, wspec, vec, vec, vec],
        out_specs=pl.BlockSpec((1, tr, c), blk),
        out_shape=jax.ShapeDtypeStruct((b, tt, c), f32),
        scratch_shapes=[pltpu.VMEM((SUBLANES, c), f32)],
        compiler_params=_cparams(("arbitrary", "arbitrary")),
        name="rglru_rev" if reverse else "rglru_fwd",
    )(xc, wa, wx, ba.reshape(1, c), bx.reshape(1, c), lam.reshape(1, c))


def _dft_kernel(t_len, x_ref, o_ref, c2, s2, rot):
    mi = pl.program_id(0)
    ki = pl.program_id(1)
    bm, bk = c2.shape
    w0 = 2.0 * math.pi / t_len

    @pl.when(ki == 0)
    def _():
        k = mi * bm + lax.broadcasted_iota(i32, (bm, bk), 0)
        a = lax.broadcasted_iota(i32, (bm, bk), 1)
        ang = ((k * a) & (t_len - 1)).astype(f32) * w0
        c2[...] = jnp.cos(ang)
        s2[...] = jnp.sin(ang)
        kcol = mi * bm + lax.broadcasted_iota(i32, (bm, 1), 0)
        angd = ((kcol * bk) & (t_len - 1)).astype(f32) * w0
        rot[0] = jnp.ones((bm, 1), f32)
        rot[1] = jnp.zeros((bm, 1), f32)
        rot[2] = jnp.cos(angd)
        rot[3] = jnp.sin(angd)
        o_ref[...] = jnp.zeros_like(o_ref)

    c1, s1, cd, sd = rot[0], rot[1], rot[2], rot[3]
    rot[0] = c1 * cd - s1 * sd
    rot[1] = s1 * cd + c1 * sd
    cm = (c1 * c2[...] - s1 * s2[...]).astype(bf16)
    sm = (s1 * c2[...] + c1 * s2[...]).astype(bf16)
    w = x_ref.shape[2]
    for b in range(x_ref.shape[0]):
        xb = x_ref[b].astype(bf16)
        o_ref[b, :, 0:w] += jnp.dot(cm, xb, preferred_element_type=f32)
        o_ref[b, :, w:2 * w] += jnp.dot(sm, xb, preferred_element_type=f32)


def dft_positions(p, col_blk, width, row0, t_len):
    b = p.shape[0]
    assert t_len & (t_len - 1) == 0
    bk = min(ROW_BLOCK, t_len)
    bm = min(1024, t_len)
    r0 = row0 // bk
    return pl.pallas_call(
        functools.partial(_dft_kernel, t_len),
        grid=(t_len // bm, t_len // bk),
        in_specs=[pl.BlockSpec((b, bk, width), lambda mi, ki: (0, r0 + ki, col_blk))],
        out_specs=pl.BlockSpec((b, bm, 2 * width), lambda mi, ki: (0, mi, 0)),
        out_shape=jax.ShapeDtypeStruct((b, t_len, 2 * width), f32),
        scratch_shapes=[pltpu.VMEM((bm, bk), f32), pltpu.VMEM((bm, bk), f32), pltpu.VMEM((4, bm, 1), f32)],
        compiler_params=_cparams(("arbitrary", "arbitrary")),
        name="dft_positions",
    )(p)


def _residual_and_ffn_norm(h, y, mod_ref, gf_ref, hout_ref, f_ref):
    hn = h + mod_ref[0, 2:3, :] * y
    hout_ref[0] = hn
    f_ref[0] = _rms(hn, gf_ref[...]) * (1.0 + mod_ref[0, 4:5, :]) + mod_ref[0, 3:4, :]


def _merge_even_kernel(nctx_blk, sc_ctx, sc_lat, hf_ref, hb_ref, gate_ref, z_ref, h_ref, mod_ref, cc_ref, cs_ref,
                       fw_ref, fb_ref, wa_ref, wb_ref, gf_ref, hout_ref, f_ref):
    i = pl.program_id(1)
    ya = (hf_ref[0] + hb_ref[0]) * _gelu_tanh(gate_ref[0])
    z = z_ref[0]
    w = z.shape[1] // 2
    zz = (jnp.dot(z[:, :w], cc_ref[...], precision=HIGHEST, preferred_element_type=f32)
          - jnp.dot(z[:, w:], cs_ref[...], precision=HIGHEST, preferred_element_type=f32))
    zz = zz * jnp.where(i < nctx_blk, sc_ctx, sc_lat)
    yb = jnp.dot(zz, fw_ref[...], precision=HIGHEST, preferred_element_type=f32) + fb_ref[...]
    y = (jnp.dot(ya.astype(bf16), wa_ref[...], preferred_element_type=f32)
         + jnp.dot(yb.astype(bf16), wb_ref[...], preferred_element_type=f32))
    _residual_and_ffn_norm(h_ref[0], y, mod_ref, gf_ref, hout_ref, f_ref)


def merge_even(hf, hb, p, z, h, mod, fw_bd, fb, w_out, gf, nctx_blk, t_ctx, t_lat):
    b, tt, d = h.shape
    tr = ROW_BLOCK
    jj, kk = np.meshgrid(np.arange(FN_GROUP_DIM), np.arange(FN_GROUP_DIM), indexing="ij")
    ang = 2.0 * np.pi * jj * kk / FN_GROUP_DIM
    eye = np.eye(FN_GROUPS)
    cc = jnp.asarray(np.kron(eye, np.cos(ang)), f32)
    cs = jnp.asarray(np.kron(eye, np.sin(ang)), f32)
    sc_ctx = 1.0 / math.sqrt(t_ctx * FN_GROUP_DIM)
    sc_lat = 1.0 / math.sqrt(t_lat * FN_GROUP_DIM)

    def rows(wd, cb=0):
        return pl.BlockSpec((1, tr, wd), lambda bi, i: (bi, i, cb))

    def full(shape):
        return pl.BlockSpec(shape, lambda bi, i: (0,) * len(shape))

    return pl.pallas_call(
        functools.partial(_merge_even_kernel, nctx_blk, sc_ctx, sc_lat),
        grid=(b, tt // tr),
        in_specs=[
            rows(RG_WIDTH), rows(RG_WIDTH), rows(RG_WIDTH, 1), rows(2 * FN_WIDTH), rows(d),
            pl.BlockSpec((1, SUBLANES, d), lambda bi, i: (_mod_row(bi, i, nctx_blk), 0, 0)),
            full((FN_WIDTH, FN_WIDTH)), full((FN_WIDTH, FN_WIDTH)), full((FN_WIDTH, FN_WIDTH)), full((1, FN_WIDTH)),
            full((RG_WIDTH, d)), full((FN_WIDTH, d)), full((1, d)),
        ],
        out_specs=[rows(d), rows(d)],
        out_shape=[jax.ShapeDtypeStruct((b, tt, d), f32)] * 2,
        compiler_params=_cparams(("arbitrary", "arbitrary")),
        name="merge_even",
    )(hf, hb, p, z, h, mod, cc, cs, fw_bd, fb.reshape(1, FN_WIDTH),
      w_out[:RG_WIDTH].astype(bf16), w_out[RG_WIDTH:].astype(bf16), gf.reshape(1, d))


def _merge_odd_kernel(hs0_ref, hs1_ref, o_ref, at_ref, h_ref, mod_ref, hg_ref, wm_ref, wa_ref, gf_ref, hout_ref, f_ref):
    hs = hs0_ref[0, 0] + hs1_ref[0, 0]
    parts = []
    for hd in range(ML_HEADS):
        sl = slice(hd * ML_HEAD_DIM, (hd + 1) * ML_HEAD_DIM)
        parts.append(_rms(hs[:, sl], hg_ref[:, sl]))
    yml = _sigmoid(o_ref[0]) * jnp.concatenate(parts, axis=1)
    y = (jnp.dot(yml.astype(bf16), wm_ref[...], preferred_element_type=f32)
         + jnp.dot(at_ref[0].astype(bf16), wa_ref[...], preferred_element_type=f32))
    _residual_and_ffn_norm(h_ref[0], y, mod_ref, gf_ref, hout_ref, f_ref)


def merge_odd(hs, p, attn, h, mod, head_g, w_ml, w_at, gf, nctx_blk):
    b, t, ap = attn.shape
    d = h.shape[2]
    tr = ROW_BLOCK

    def rows(wd, cb=0, off=0):
        return pl.BlockSpec((1, tr, wd), lambda bi, i: (bi, i + off, cb))

    def full(shape):
        return pl.BlockSpec(shape, lambda bi, i: (0,) * len(shape))

    return pl.pallas_call(
        _merge_odd_kernel,
        grid=(b, t // tr),
        in_specs=[
            pl.BlockSpec((1, 1, tr, ML_WIDTH), lambda bi, i: (0, bi, i + nctx_blk, 0)),
            pl.BlockSpec((1, 1, tr, ML_WIDTH), lambda bi, i: (1, bi, i + nctx_blk, 0)),
            rows(ML_WIDTH, OD_O // ML_WIDTH, nctx_blk), rows(ap), rows(d, 0, nctx_blk),
            pl.BlockSpec((1, SUBLANES, d), lambda bi, i: (bi, 0, 0)),
            full((1, ML_WIDTH)), full((ML_WIDTH, d)), full((ap, d)), full((1, d)),
        ],
        out_specs=[rows(d), rows(d)],
        out_shape=[jax.ShapeDtypeStruct((b, t, d), f32)] * 2,
        compiler_params=_cparams(("arbitrary", "arbitrary")),
        name="merge_odd",
    )(hs, hs, p, attn, h, mod, head_g.reshape(1, ML_WIDTH), w_ml, w_at, gf.reshape(1, d))


def _mlstm_kernel(qk_ref, v_ref, g_ref, gt_ref, gb_ref, gbt_ref, o_ref, ct, nst, mst):
    d = pl.program_id(0)
    c = pl.program_id(2)
    L = ML_CHUNK

    @pl.when(c == 0)
    def _():
        ct[...] = jnp.zeros_like(ct)
        nst[...] = jnp.zeros_like(nst)
        mst[...] = jnp.zeros_like(mst)

    ri = lax.broadcasted_iota(i32, (L, L), 0)
    ci = lax.broadcasted_iota(i32, (L, L), 1)
    sgn = 1 - 2 * d
    tri = (ci - ri) * sgn <= 0
    trit = (ri - ci) * sgn <= 0
    gcol = g_ref[0] + gb_ref[...]
    grow = gt_ref[0] + gbt_ref[...]
    lane = lax.broadcasted_iota(i32, (1, LANES), 1)
    for hd in range(ML_HEADS):
        sl = slice(hd * ML_HEAD_DIM, (hd + 1) * ML_HEAD_DIM)
        q = qk_ref[0, :, sl] * (ML_HEAD_DIM ** -0.5)
        k = qk_ref[0, :, ML_WIDTH + hd * ML_HEAD_DIM:ML_WIDTH + (hd + 1) * ML_HEAD_DIM]
        v = v_ref[0, :, sl]
        li = d * 8 + hd
        ig_col = jnp.sum(jnp.where(lane == li, gcol, 0.0), axis=1, keepdims=True)
        fg_col = jnp.sum(jnp.where(lane == li + 4, gcol, 0.0), axis=1, keepdims=True)
        rsel = lax.broadcasted_iota(i32, (ML_N_GATES, 1), 0)
        ig_row = jnp.sum(jnp.where(rsel == li, grow, 0.0), axis=0, keepdims=True)
        fg_row = jnp.sum(jnp.where(rsel == li + 4, grow, 0.0), axis=0, keepdims=True)
        lf_col = -_softplus(-fg_col)
        lf_row = -_softplus(-fg_row)
        b_col = jnp.sum(jnp.where(tri, lf_row, 0.0), axis=1, keepdims=True)
        b_row = jnp.sum(jnp.where(trit, lf_col, 0.0), axis=0, keepdims=True)
        m_prev = mst[hd, 0:1, 0:1]
        n_prev = nst[hd, 0:1, :]
        ct_prev = ct[hd]
        dm = jnp.where(tri, b_col - b_row + ig_row, -jnp.inf)
        inter = b_col + m_prev
        m_row = jnp.maximum(jnp.max(dm, axis=1, keepdims=True), inter)
        qb, kb, vb = q.astype(bf16), k.astype(bf16), v.astype(bf16)
        s = lax.dot_general(qb, kb, (((1,), (1,)), ((), ())), preferred_element_type=f32) * jnp.exp(dm - m_row)
        w_inter = jnp.exp(inter - m_row)
        num = (jnp.dot(s.astype(bf16), vb, preferred_element_type=f32)
               + w_inter * jnp.dot(qb, ct_prev.astype(bf16), preferred_element_type=f32))
        den = jnp.sum(s, axis=1, keepdims=True) + w_inter * jnp.sum(q * n_prev, axis=1, keepdims=True)
        o_ref[0, 0, :, sl] = num / jnp.maximum(jnp.abs(den), jnp.exp(-m_row))
        b_l = jnp.sum(lf_col, axis=0, keepdims=True)
        log_wk = b_l - b_col + ig_col
        m_new = jnp.maximum(b_l + m_prev, jnp.max(log_wk, axis=0, keepdims=True))
        kw = jnp.exp(log_wk - m_new) * k
        decay = jnp.exp(b_l + m_prev - m_new)
        ct[hd] = decay * ct_prev + lax.dot_general(kw.astype(bf16), vb, (((0,), (0,)), ((), ())),
                                                   preferred_element_type=f32)
        nst[hd, 0:1, :] = decay * n_prev + jnp.sum(kw, axis=0, keepdims=True)
        mst[hd] = jnp.broadcast_to(m_new, (SUBLANES, LANES))


def mlstm(qkc, p, gt, gate_b, nctx_chunks):
    b, tt, _ = qkc.shape
    L = ML_CHUNK
    nch = tt // L

    def chunk(d, c):
        return jnp.where(d == 0, c, jnp.where(c < nctx_chunks, nctx_chunks - 1 - c, nch - 1 - (c - nctx_chunks)))

    gb = jnp.zeros((1, LANES), f32).at[0, :ML_N_GATES].set(gate_b)
    return pl.pallas_call(
        _mlstm_kernel,
        grid=(2, b, nch),
        in_specs=[
            pl.BlockSpec((1, L, 2 * ML_WIDTH), lambda d, bi, c: (bi, chunk(d, c), 0)),
            pl.BlockSpec((1, L, ML_WIDTH), lambda d, bi, c: (bi, chunk(d, c), OD_V // ML_WIDTH)),
            pl.BlockSpec((1, L, LANES), lambda d, bi, c: (bi, chunk(d, c), OD_G // LANES)),
            pl.BlockSpec((1, ML_N_GATES, L), lambda d, bi, c: (bi, 0, chunk(d, c))),
            pl.BlockSpec((1, LANES), lambda d, bi, c: (0, 0)),
            pl.BlockSpec((ML_N_GATES, 1), lambda d, bi, c: (0, 0)),
        ],
        out_specs=pl.BlockSpec((1, 1, L, ML_WIDTH), lambda d, bi, c: (d, bi, chunk(d, c), 0)),
        out_shape=jax.ShapeDtypeStruct((2, b, tt, ML_WIDTH), f32),
        scratch_shapes=[
            pltpu.VMEM((ML_HEADS, ML_HEAD_DIM, ML_HEAD_DIM), f32),
            pltpu.VMEM((ML_HEADS, SUBLANES, LANES), f32),
            pltpu.VMEM((ML_HEADS, SUBLANES, LANES), f32),
        ],
        compiler_params=_cparams(("arbitrary", "arbitrary", "arbitrary")),
        name="mlstm",
    )(qkc, p, p, gt, gb, gate_b.reshape(ML_N_GATES, 1))


def _mla_prep_kernel(nctx_blk, p_ref, qg_ref, kg_ref, wq_ref, wqs_ref, wk_ref, wv_ref, em_ref, es_ref,
                     q_ref, k_ref, v_ref):
    i = pl.program_id(1)
    tr = p_ref.shape[1]
    cq = _rms(p_ref[0, :, OD_CQ:OD_CKV], qg_ref[...]).astype(bf16)
    ckv = _rms(p_ref[0, :, OD_CKV:OD_KR], kg_ref[...]).astype(bf16)
    kr = p_ref[0, :, OD_KR:OD_N]
    lane = lax.broadcasted_iota(i32, (1, MLA_HEAD_PAD), 1)
    half = MLA_ROPE // 2
    pair = jnp.where(lane < MLA_NOPE + half, lane - MLA_NOPE, lane - MLA_NOPE - half)
    is_rope = jnp.logical_and(lane >= MLA_NOPE, lane < MLA_NOPE + MLA_ROPE)
    use_row = pair < half // 2
    fidx = jnp.clip(jnp.where(use_row, pair, pair - half // 2), 0, half // 2 - 1).astype(f32)
    inv = jnp.exp(fidx * (-math.log(ROPE_THETA) / (half // 2)))
    t = (i - nctx_blk) * tr + lax.broadcasted_iota(i32, (tr, 1), 0)
    pos = jnp.where(use_row, (t >> GRID_SHIFT).astype(f32), (t & (GRID_W - 1)).astype(f32))
    ang = pos * inv
    rot = jnp.logical_and(is_rope, i >= nctx_blk)
    cos1 = jnp.where(rot, jnp.cos(ang), 1.0)
    sin1 = jnp.where(rot, jnp.sin(ang), 0.0)
    cos_t = jnp.concatenate([cos1] * MLA_HEADS, axis=1)
    sin_t = jnp.concatenate([sin1] * MLA_HEADS, axis=1)
    scale = (MLA_NOPE + MLA_ROPE) ** -0.5
    q = (jnp.dot(cq, wq_ref[...], preferred_element_type=f32) * cos_t
         + jnp.dot(cq, wqs_ref[...], preferred_element_type=f32) * sin_t)
    q_ref[0] = (q * scale).astype(bf16)
    k = (jnp.dot(ckv, wk_ref[...], preferred_element_type=f32)
         + jnp.dot(kr, em_ref[...], precision=HIGHEST, preferred_element_type=f32) * cos_t
         + jnp.dot(kr, es_ref[...], precision=HIGHEST, preferred_element_type=f32) * sin_t)
    k_ref[0] = k.astype(bf16)
    v_ref[0] = jnp.dot(ckv, wv_ref[...], preferred_element_type=f32).astype(bf16)


def _mla_weights(w_uq, w_ukv):
    hp = MLA_HEAD_PAD
    half = MLA_ROPE // 2
    wq = w_uq.reshape(MLA_Q_LORA, MLA_HEADS, MLA_NOPE + MLA_ROPE)
    nope, x1, x2 = wq[..., :MLA_NOPE], wq[..., MLA_NOPE::2], wq[..., MLA_NOPE + 1::2]
    zq = jnp.zeros((MLA_Q_LORA, MLA_HEADS, hp - MLA_NOPE - MLA_ROPE), f32)
    wq_main = jnp.concatenate([nope, x1, x2, zq], axis=-1).reshape(MLA_Q_LORA, MLA_HEADS * hp)
    wq_swap = jnp.concatenate([jnp.zeros_like(nope), -x2, x1, zq], axis=-1).reshape(MLA_Q_LORA, MLA_HEADS * hp)
    wkv = w_ukv.reshape(MLA_KV_LORA, MLA_HEADS, MLA_NOPE + MLA_V)
    zk = jnp.zeros((MLA_KV_LORA, MLA_HEADS, hp - MLA_NOPE), f32)
    wk = jnp.concatenate([wkv[..., :MLA_NOPE], zk], axis=-1).reshape(MLA_KV_LORA, MLA_HEADS * hp)
    wv = jnp.concatenate([wkv[..., MLA_NOPE:], jnp.zeros((MLA_KV_LORA, MLA_HEADS, hp - MLA_V), f32)],
                         axis=-1).reshape(MLA_KV_LORA, MLA_HEADS * hp)
    em = np.zeros((LANES, MLA_HEADS, hp), np.float32)
    es = np.zeros((LANES, MLA_HEADS, hp), np.float32)
    for j in range(half):
        em[2 * j, :, MLA_NOPE + j] = 1.0
        em[2 * j + 1, :, MLA_NOPE + half + j] = 1.0
        es[2 * j + 1, :, MLA_NOPE + j] = -1.0
        es[2 * j, :, MLA_NOPE + half + j] = 1.0
    em = jnp.asarray(em.reshape(LANES, MLA_HEADS * hp))
    es = jnp.asarray(es.reshape(LANES, MLA_HEADS * hp))
    return wq_main.astype(bf16), wq_swap.astype(bf16), wk.astype(bf16), wv.astype(bf16), em, es


def mla_prep(p, q_norm_g, kv_norm_g, w_uq, w_ukv, nctx_blk):
    b, tt, n = p.shape
    tr = ROW_BLOCK
    hw = MLA_HEADS * MLA_HEAD_PAD
    wq, wqs, wk, wv, em, es = _mla_weights(w_uq, w_ukv)

    def full(shape):
        return pl.BlockSpec(shape, lambda bi, i: (0,) * len(shape))

    out = pl.BlockSpec((1, tr, hw), lambda bi, i: (bi, i, 0))
    return pl.pallas_call(
        functools.partial(_mla_prep_kernel, nctx_blk),
        grid=(b, tt // tr),
        in_specs=[pl.BlockSpec((1, tr, n), lambda bi, i: (bi, i, 0)),
                  full((1, MLA_Q_LORA)), full((1, MLA_KV_LORA)),
                  full((MLA_Q_LORA, hw)), full((MLA_Q_LORA, hw)), full((MLA_KV_LORA, hw)), full((MLA_KV_LORA, hw)),
                  full((LANES, hw)), full((LANES, hw))],
        out_specs=[out, out, out],
        out_shape=[jax.ShapeDtypeStruct((b, tt, hw), bf16)] * 3,
        compiler_params=_cparams(("arbitrary", "arbitrary")),
        name="mla_prep",
    )(p, q_norm_g.reshape(1, MLA_Q_LORA), kv_norm_g.reshape(1, MLA_KV_LORA), wq, wqs, wk, wv, em, es)


def _mla_attn_kernel(q_ref, k_ref, v_ref, o_ref):
    tq = q_ref.shape[1]
    cq = tq // ATTN_Q_CHUNKS
    for c in range(ATTN_Q_CHUNKS):
        rows = slice(c * cq, (c + 1) * cq)
        s = lax.dot_general(q_ref[0, rows, :], k_ref[0], (((1,), (1,)), ((), ())), preferred_element_type=f32)
        m = jnp.max(s, axis=-1, keepdims=True)
        e = jnp.exp(s - m)
        l = jnp.sum(e, axis=-1, keepdims=True)
        o_ref[0, rows, :] = jnp.dot(e.astype(bf16), v_ref[0], preferred_element_type=f32) / l


def mla_attention(q, k, v, nctx_blk):
    b, tt, hw = q.shape
    tq = ROW_BLOCK
    nq = tt // tq - nctx_blk
    hp = MLA_HEAD_PAD
    kv = pl.BlockSpec((1, tt, hp), lambda bi, h, qi: (bi, 0, h))
    return pl.pallas_call(
        _mla_attn_kernel,
        grid=(b, MLA_HEADS, nq),
        in_specs=[pl.BlockSpec((1, tq, hp), lambda bi, h, qi: (bi, qi + nctx_blk, h)), kv, kv],
        out_specs=pl.BlockSpec((1, tq, hp), lambda bi, h, qi: (bi, qi, h)),
        out_shape=jax.ShapeDtypeStruct((b, nq * tq, hw), f32),
        compiler_params=_cparams(("arbitrary", "arbitrary", "arbitrary")),
        name="mla_attention",
    )(q, k, v)


def _topk_rows(s, k, payload=None):
    n = s.shape[0]
    rid = lax.broadcasted_iota(i32, s.shape, 0)
    vals, outs = [], []
    for _ in range(k):
        m = jnp.max(s, axis=0, keepdims=True)
        am = jnp.min(jnp.where(s == m, rid, n), axis=0, keepdims=True)
        hit = rid == am
        vals.append(m)
        outs.append(am if payload is None else jnp.sum(jnp.where(hit, payload, 0), axis=0, keepdims=True))
        s = jnp.where(hit, -jnp.inf, s)
    return jnp.concatenate(vals, axis=0), jnp.concatenate(outs, axis=0)


def _pruned_candidates(vals, idxs):
    k = PEER_TOPK
    hk = k // 2
    bcol = lax.broadcasted_iota(i32, (hk, 1), 0)
    cand = [vals[0][0:1, :] + vals[1]]
    cidx = [idxs[0][0:1, :] * PEER_N_KEYS + idxs[1]]
    for a in range(1, hk):
        ok = (a + 1) * (bcol + 1) <= k
        cand.append(jnp.where(ok, vals[0][a:a + 1, :] + vals[1][0:hk, :], -jnp.inf))
        cidx.append(idxs[0][a:a + 1, :] * PEER_N_KEYS + idxs[1][0:hk, :])
    cand.append(vals[0][hk:k, :] + vals[1][0:1, :])
    cidx.append(idxs[0][hk:k, :] * PEER_N_KEYS + idxs[1][0:1, :])
    return jnp.concatenate(cand, axis=0), jnp.concatenate(cidx, axis=0)


def _peer_route_kernel(f_ref, wq_ref, keys_ref, code_ref, g_ref):
    q = jnp.dot(f_ref[...].astype(bf16), wq_ref[...], preferred_element_type=f32)
    hk = PEER_KEY_DIM // 2
    for h in range(PEER_HEADS):
        vals, idxs = [], []
        for p in range(2):
            c0 = (h * 2 + p) * hk
            qhp = q[:, c0:c0 + hk].astype(bf16)
            st = lax.dot_general(keys_ref[p, h], qhp, (((1,), (1,)), ((), ())), preferred_element_type=f32)
            v, ix = _topk_rows(st, PEER_TOPK)
            vals.append(v)
            idxs.append(ix)
        cand, cidx = _pruned_candidates(vals, idxs)
        top_s, e = _topk_rows(cand, PEER_TOPK, payload=cidx)
        ex = jnp.exp(top_s - top_s[0:1, :])
        rows = slice(h * PEER_TOPK, (h + 1) * PEER_TOPK)
        g_ref[rows, :] = ex / jnp.sum(ex, axis=0, keepdims=True)
        code_ref[rows, :] = (e >> 1) * SUBLANES + ((1 - (e & 1)) << LOW_HALF_BIT)


def peer_route(f2, wq, keys):
    r, d = f2.shape
    tb = PEER_TB
    out = pl.BlockSpec((PEER_PICKS, tb), lambda i: (0, i))
    return pl.pallas_call(
        _peer_route_kernel,
        grid=(r // tb,),
        in_specs=[
            pl.BlockSpec((tb, d), lambda i: (i, 0)),
            pl.BlockSpec(wq.shape, lambda i: (0, 0)),
            pl.BlockSpec(keys.shape, lambda i: (0, 0, 0, 0)),
        ],
        out_specs=[out, out],
        out_shape=[jax.ShapeDtypeStruct((PEER_PICKS, r), i32), jax.ShapeDtypeStruct((PEER_PICKS, r), f32)],
        compiler_params=_cparams(("arbitrary",)),
        name="peer_route",
    )(f2, wq, keys)


def pack_expert_table(t):
    e, d = t.shape
    bits = lax.bitcast_convert_type(t.astype(bf16), jnp.uint16).astype(jnp.uint32)
    bits = bits.reshape(e // 2, 2, SUBLANES, d // SUBLANES)
    word = bits[:, 0] | (bits[:, 1] << 16)
    return lax.bitcast_convert_type(word, i32).reshape(e // 2 * SUBLANES, d // SUBLANES)


def _load_table_once(tab_hbm, tab_vmem, sem):
    @pl.when(pl.program_id(0) == 0)
    def _():
        cp = pltpu.make_async_copy(tab_hbm, tab_vmem, sem)
        cp.start()
        cp.wait()


def _slot_major(a, tb):
    r = a.shape[0]
    ng = PEER_PICKS // SUBLANES
    return a.reshape(r // tb, tb, ng, SUBLANES).transpose(0, 3, 1, 2).reshape(r // tb, SUBLANES, tb * ng)


def _fetch_block(src_hbm, dst_smems, sems):
    i = pl.program_id(0)
    n = dst_smems[0].shape[0] // 2
    half = i % 2

    def copies(blk, h):
        off = pl.multiple_of(h * n, n)
        return [pltpu.make_async_copy(src_hbm.at[blk, s], dst_smems[s].at[pl.ds(off, n)], sems.at[h, s])
                for s in range(SUBLANES)]

    @pl.when(i == 0)
    def _():
        for cp in copies(0, 0):
            cp.start()

    for cp in copies(i, half):
        cp.wait()

    @pl.when(i + 1 < pl.num_programs(0))
    def _():
        for cp in copies(i + 1, 1 - half):
            cp.start()

    return half * n


def _rotate_tiles(tb, mbufs, fill, consume):
    nb = len(mbufs)

    def rotate(p, c):
        for q in range(nb):
            consume(p * nb + q, mbufs[q])
            fill((p + 1) * nb + q, mbufs[q])
        return c

    for q in range(nb):
        fill(q, mbufs[q])
    lax.fori_loop(0, tb // nb - 1, rotate, 0)
    for q in range(nb):
        consume(tb - nb + q, mbufs[q])


def _peer_act_kernel(code_hbm, x_ref, g_ref, tab_hbm, ex_ref, rd_ref, wx_ref, tab_vmem, abuf, sem_tab, sem_a, *bufs):
    tb = x_ref.shape[0]
    ng = PEER_PICKS // SUBLANES
    nr = 2 * SUBLANES
    nx = PEER_PICKS // 2 * nr
    nb = PEER_TILE_DEPTH
    mbufs, code_s = bufs[:nb], bufs[nb:]
    _load_table_once(tab_hbm, tab_vmem, sem_tab)
    base = _fetch_block(code_hbm, code_s, sem_a)
    lane = lax.broadcasted_iota(i32, (nr, nx), 1)
    sub = lax.broadcasted_iota(i32, (nr, nx), 0)
    own = (lane & (nr - 1)) == sub
    half = SUBLANES // 2

    def pack_pairs(t, mbuf):
        for jg in range(ng):
            g = base + t * ng + jg
            for s in range(half):
                ca, cb = code_s[s][g], code_s[s + half][g]
                wa = tab_vmem[pl.ds(pl.multiple_of(ca & ROW_MASK, SUBLANES), SUBLANES), :]
                wb = tab_vmem[pl.ds(pl.multiple_of(cb & ROW_MASK, SUBLANES), SUBLANES), :]
                sha = 16 - (jnp.full((SUBLANES, LANES), ca, i32) >> 16)
                shb = jnp.full((SUBLANES, LANES), cb, i32) >> 16
                lo = lax.shift_right_logical(wa, sha) & 0xFFFF
                mbuf[pl.ds((jg * half + s) * SUBLANES, SUBLANES), :] = lo | ((wb << shb) & HI_MASK)

    def dots(t, mbuf):
        gm = lax.dot_general(x_ref[t], pltpu.bitcast(mbuf[...], bf16), (((1,), (1,)), ((), ())),
                             preferred_element_type=f32)
        abuf[pl.ds(t, 1), :] = jnp.sum(jnp.where(own, gm, 0.0), axis=0, keepdims=True)

    _rotate_tiles(tb, mbufs, pack_pairs, dots)
    a = abuf[...]
    a_hi = a.astype(bf16)
    a_lo = (a - a_hi.astype(f32)).astype(bf16)
    act = jnp.dot(a_hi, rd_ref[...], preferred_element_type=f32) + jnp.dot(a_lo, rd_ref[...], preferred_element_type=f32)
    w = g_ref[0] * _gelu_tanh(act)
    low = g_ref[1] > 0.5
    wx_ref[...] = (jnp.dot(jnp.where(low, w, 0.0).astype(bf16), ex_ref[0], preferred_element_type=f32)
                   + jnp.dot(jnp.where(low, 0.0, w).astype(bf16), ex_ref[1], preferred_element_type=f32))


def _expansion_matrices():
    ex = np.zeros((2, PEER_PICKS, PEER_PICKS * 2 * SUBLANES), np.float32)
    for j in range(PEER_PICKS):
        for s in range(SUBLANES):
            ex[0, j, j * 2 * SUBLANES + 2 * s] = 1.0
            ex[1, j, j * 2 * SUBLANES + 2 * s + 1] = 1.0
    return ex


def _pair_reduction_matrix():
    half = SUBLANES // 2
    rd = np.zeros((PEER_PICKS // 2 * 2 * SUBLANES, PEER_PICKS), np.float32)
    for jg in range(PEER_PICKS // SUBLANES):
        for s in range(SUBLANES):
            p = jg * half + s % half
            for q in range(SUBLANES):
                rd[p * 2 * SUBLANES + 2 * q + (s >= half), jg * SUBLANES + s] = 1.0
    return rd


def peer_act(code_sm, x2, glow, tab):
    r = x2.shape[0]
    tb = PEER_TB
    ng = PEER_PICKS // SUBLANES
    nx = PEER_PICKS * 2 * SUBLANES
    npair = PEER_PICKS // 2
    anyspec = pl.BlockSpec(memory_space=pl.ANY)
    return pl.pallas_call(
        _peer_act_kernel,
        grid=(r // tb,),
        in_specs=[anyspec, pl.BlockSpec((tb, 2 * SUBLANES, LANES), lambda i: (i, 0, 0)),
                  pl.BlockSpec((2, tb, PEER_PICKS), lambda i: (0, i, 0)), anyspec,
                  pl.BlockSpec((2, PEER_PICKS, nx), lambda i: (0, 0, 0)),
                  pl.BlockSpec((npair * 2 * SUBLANES, PEER_PICKS), lambda i: (0, 0))],
        out_specs=pl.BlockSpec((tb, nx), lambda i: (i, 0)),
        out_shape=jax.ShapeDtypeStruct((r, nx), f32),
        scratch_shapes=[
            pltpu.VMEM(tab.shape, i32),
            pltpu.VMEM((tb, npair * 2 * SUBLANES), f32),
            pltpu.SemaphoreType.DMA, pltpu.SemaphoreType.DMA((2, SUBLANES)),
        ] + [pltpu.VMEM((npair * SUBLANES, LANES), i32)] * PEER_TILE_DEPTH
          + [pltpu.SMEM((2 * tb * ng,), i32)] * SUBLANES,
        compiler_params=_cparams(("arbitrary",), VMEM_LIMIT_TABLE),
        name="peer_act",
    )(code_sm, x2, glow, tab, jnp.asarray(_expansion_matrices(), bf16), jnp.asarray(_pair_reduction_matrix(), bf16))


def _peer_out_kernel(row_hbm, wx_ref, tab_hbm, o_ref, tab_vmem, sem_tab, sem_a, *bufs):
    i = pl.program_id(0)
    tb = o_ref.shape[0]
    ng = PEER_PICKS // SUBLANES
    nx = wx_ref.shape[1]
    nb = PEER_TILE_DEPTH
    mbufs, row_s = bufs[:nb], bufs[nb:]
    _load_table_once(tab_hbm, tab_vmem, sem_tab)
    row_base = _fetch_block(row_hbm, row_s, sem_a)
    lane = lax.broadcasted_iota(i32, (SUBLANES, nx), 1)
    sub = lax.broadcasted_iota(i32, (SUBLANES, nx), 0)
    diag = ((lane & (2 * SUBLANES - 1)) >> 1) == sub

    def contract(t, mbuf):
        wsel = jnp.where(diag, jnp.broadcast_to(wx_ref[pl.ds(t, 1), :], (SUBLANES, nx)), 0.0).astype(bf16)
        o_ref[t] = jnp.dot(wsel, pltpu.bitcast(mbuf[...], bf16), preferred_element_type=f32)

    def copy_tiles(t, mbuf):
        for jg in range(ng):
            g = row_base + t * ng + jg
            for s in range(SUBLANES):
                row = pl.multiple_of(row_s[s][g], SUBLANES)
                mbuf[pl.ds((jg * SUBLANES + s) * SUBLANES, SUBLANES), :] = tab_vmem[pl.ds(row, SUBLANES), :]

    _rotate_tiles(tb, mbufs, copy_tiles, contract)


def peer_out(row_sm, wx, tab):
    nblk, _, n = row_sm.shape
    ng = PEER_PICKS // SUBLANES
    tb = n // ng
    nx = wx.shape[1]
    assert tb % PEER_TILE_DEPTH == 0
    anyspec = pl.BlockSpec(memory_space=pl.ANY)
    return pl.pallas_call(
        _peer_out_kernel,
        grid=(nblk,),
        in_specs=[anyspec, pl.BlockSpec((tb, nx), lambda i: (i, 0)), anyspec],
        out_specs=pl.BlockSpec((tb, SUBLANES, LANES), lambda i: (i, 0, 0)),
        out_shape=jax.ShapeDtypeStruct((nblk * tb, SUBLANES, LANES), f32),
        scratch_shapes=[
            pltpu.VMEM(tab.shape, i32),
            pltpu.SemaphoreType.DMA, pltpu.SemaphoreType.DMA((2, SUBLANES)),
        ] + [pltpu.VMEM((PEER_PICKS * SUBLANES, LANES), i32)] * PEER_TILE_DEPTH
          + [pltpu.SMEM((2 * n,), i32)] * SUBLANES,
        compiler_params=_cparams(("arbitrary",), VMEM_LIMIT_TABLE),
        name="peer_out",
    )(row_sm, wx, tab)


def peer_ffn(f, wq, keys, u, v):
    b, r, d = f.shape
    f2 = f.reshape(b * r, d)
    codet, gt = peer_route(f2, wq.astype(bf16), keys.astype(bf16))
    code = codet.T
    low = ((code >> LOW_HALF_BIT) & 1).astype(f32)
    x2 = jnp.repeat(f2.reshape(b * r, SUBLANES, d // SUBLANES), 2, axis=1).astype(bf16)
    wx = peer_act(_slot_major(code, PEER_TB), x2, jnp.stack([gt.T, low]), pack_expert_table(u))
    o = peer_out(_slot_major(code & ROW_MASK, PEER_TB), wx, pack_expert_table(v))
    return o.reshape(b, r, d)


def _final_kernel(h_ref, peer_ref, mod_ref, g_ref, o_ref):
    h = h_ref[0] + mod_ref[0, 5:6, :] * peer_ref[0]
    o_ref[0] = _rms(h, g_ref[...])


def final_norm(h, peer, mod, g):
    b, t, d = h.shape
    tr = ROW_BLOCK
    row = pl.BlockSpec((1, tr, d), lambda bi, i: (bi, i, 0))
    return pl.pallas_call(
        _final_kernel,
        grid=(b, t // tr),
        in_specs=[row, row, pl.BlockSpec((1, SUBLANES, d), lambda bi, i: (bi, 0, 0)),
                  pl.BlockSpec((1, d), lambda bi, i: (0, 0))],
        out_specs=row,
        out_shape=jax.ShapeDtypeStruct((b, t, d), f32),
        compiler_params=_cparams(("arbitrary", "arbitrary")),
        name="final_norm",
    )(h, peer, mod, g.reshape(1, d))


def _block_diag(w):
    g, a, c = w.shape
    out = jnp.zeros((g * a, g * c), w.dtype)
    for k in range(g):
        out = out.at[k * a:(k + 1) * a, k * c:(k + 1) * c].set(w[k])
    return out


def kernel(x, c, ctx, c_ctx, mod_w, mod_b, norm_mix_g, norm_ffn_g, ev_w_in, ev_w_out, rg_conv_w, rg_conv_b, rg_wa, rg_ba, rg_wx, rg_bx, rg_lam, fn_w, fn_b, od_w_in, od_w_out, ml_conv_w, ml_conv_b, ml_gate_b, ml_head_g, mla_q_norm_g, mla_w_uq, mla_kv_norm_g, mla_w_ukv, peer_w_q, peer_keys, peer_u, peer_v, final_g):
    b, t, d = x.shape
    tc = ctx.shape[1]
    assert b == 2 and tc % ROW_BLOCK == 0 and t % ROW_BLOCK == 0 and mod_w.shape[0] == 2
    nctx_blk = tc // ROW_BLOCK

    cvec = jnp.zeros((SUBLANES, d), f32).at[0:b].set(c).at[b].set(c_ctx)
    mods = modulation(cvec, mod_w, mod_b)
    mods = mods.reshape(2, SUBLANES, 6, d)[:, :3]
    mods = jnp.pad(mods, ((0, 0), (0, 0), (0, 2), (0, 0)))

    h0 = jnp.concatenate([ctx, x], axis=1)

    p = inproj(h0, mods[0], norm_mix_g[0], ev_w_in[0].astype(bf16), nctx_blk)
    xc = dwconv(p, RG_WIDTH, rg_conv_w[0], rg_conv_b[0], nctx_blk, apply_silu=False)
    hf = rglru_scan(xc, rg_wa[0, 0], rg_wx[0, 0], rg_ba[0, 0], rg_bx[0, 0], rg_lam[0, 0], nctx_blk, reverse=False)
    hb = rglru_scan(xc, rg_wa[0, 1], rg_wx[0, 1], rg_ba[0, 1], rg_bx[0, 1], rg_lam[0, 1], nctx_blk, reverse=True)
    fcol = 2 * RG_WIDTH // FN_WIDTH
    z = jnp.concatenate([dft_positions(p, fcol, FN_WIDTH, 0, tc), dft_positions(p, fcol, FN_WIDTH, tc, t)], axis=1)
    h1, f1 = merge_even(hf, hb, p, z, h0, mods[0], _block_diag(fn_w[0]), fn_b[0], ev_w_out[0], norm_ffn_g[0],
                        nctx_blk, tc, t)
    peer1 = peer_ffn(f1, peer_w_q[0], peer_keys[0], peer_u[0], peer_v[0])

    w = od_w_in[0]
    zpad = lambda n: jnp.zeros((d, n), f32)
    w_od = jnp.concatenate([w[:, :OD_G + ML_N_GATES], zpad(OD_CQ - OD_G - ML_N_GATES),
                            w[:, 2064:2064 + MLA_Q_LORA + MLA_KV_LORA + MLA_ROPE],
                            zpad(OD_N - OD_KR - MLA_ROPE)], axis=1).astype(bf16)
    h2, p2 = inproj(h1, mods[1], norm_mix_g[1], w_od, nctx_blk, peer=peer1, mod_prev=mods[0])
    qkc = dwconv(p2, 2 * ML_WIDTH, ml_conv_w[0], ml_conv_b[0], nctx_blk, apply_silu=True)
    gt = jnp.swapaxes(p2[:, :, OD_G:OD_G + ML_N_GATES], 1, 2)
    hs = mlstm(qkc, p2, gt, ml_gate_b[0], tc // ML_CHUNK)
    q, k, v = mla_prep(p2, mla_q_norm_g[0], mla_kv_norm_g[0], mla_w_uq[0], mla_w_ukv[0], nctx_blk)
    attn = mla_attention(q, k, v, nctx_blk)
    wo = od_w_out[0]
    w_at = jnp.concatenate([wo[ML_WIDTH:].reshape(MLA_HEADS, MLA_V, d),
                            jnp.zeros((MLA_HEADS, MLA_HEAD_PAD - MLA_V, d), f32)], axis=1)
    w_at = w_at.reshape(MLA_HEADS * MLA_HEAD_PAD, d).astype(bf16)
    h3, f3 = merge_odd(hs, p2, attn, h2, mods[1, :b], ml_head_g[0], wo[:ML_WIDTH].astype(bf16), w_at,
                       norm_ffn_g[1], nctx_blk)
    peer3 = peer_ffn(f3, peer_w_q[1], peer_keys[1], peer_u[1], peer_v[1])
    return final_norm(h3, peer3, mods[1, :b], final_g)
```

```python
import functools
import math

import numpy as np
import jax
import jax.numpy as jnp
from jax import lax
from jax.experimental import pallas as pl
from jax.experimental.pallas import tpu as pltpu

f32 = jnp.float32
bf16 = jnp.bfloat16
i32 = jnp.int32
HIGHEST = lax.Precision.HIGHEST

SUBLANES = 8
LANES = 128
VMEM_LIMIT = 48 * 1024 * 1024
VMEM_LIMIT_TABLE = 56 * 1024 * 1024

EPS = 1e-6
GRID_W = 64
GRID_SHIFT = 6
ROW_BLOCK = 256
CONV_W = 4
RG_HEADS, RG_HEAD_DIM = 6, 128
RG_WIDTH = RG_HEADS * RG_HEAD_DIM
RG_C = 8.0
FN_GROUPS, FN_GROUP_DIM = 4, 64
FN_WIDTH = FN_GROUPS * FN_GROUP_DIM
ML_HEADS, ML_HEAD_DIM = 4, 128
ML_WIDTH = ML_HEADS * ML_HEAD_DIM
ML_CHUNK = 128
ML_N_GATES = 2 * 2 * ML_HEADS
MLA_HEADS, MLA_NOPE, MLA_ROPE, MLA_V = 8, 64, 32, 64
MLA_Q_LORA, MLA_KV_LORA = 384, 256
MLA_HEAD_PAD = 128
ATTN_Q_CHUNKS = 2
ROPE_THETA = 10000.0
PEER_HEADS, PEER_KEY_DIM, PEER_N_KEYS, PEER_TOPK = 8, 256, 128, 16
PEER_PICKS = PEER_HEADS * PEER_TOPK
PEER_TB = 128
PEER_TILE_DEPTH = 8
ROW_MASK = 0xFFF8
HI_MASK = -65536
LOW_HALF_BIT = 20

OD_V, OD_O, OD_G, OD_CQ, OD_CKV, OD_KR, OD_N = 1024, 1536, 2048, 2176, 2560, 2816, 2944


def _cparams(sem, vmem=VMEM_LIMIT):
    return pltpu.CompilerParams(dimension_semantics=sem, vmem_limit_bytes=vmem)


def _gelu_tanh(x):
    return 0.5 * x * (1.0 + jnp.tanh(0.7978845608028654 * (x + 0.044715 * x * x * x)))


def _sigmoid(x):
    return 1.0 / (1.0 + jnp.exp(-x))


def _softplus(x):
    return jnp.maximum(x, 0.0) + jnp.log1p(jnp.exp(-jnp.abs(x)))


def _rms(x, g):
    return x * lax.rsqrt(jnp.mean(x * x, axis=-1, keepdims=True) + EPS) * g


def _mod_row(b, i, nctx_blk):
    return jnp.where(i < nctx_blk, 2, b)


def _mod_kernel(c_ref, w_ref, b_ref, o_ref):
    c = c_ref[...]
    s = c * _sigmoid(c)
    o_ref[0] = jnp.dot(s, w_ref[0], precision=HIGHEST, preferred_element_type=f32) + b_ref[0]


def modulation(cvec, mod_w, mod_b):
    depth, d, n = mod_w.shape
    tn = 1536
    return pl.pallas_call(
        _mod_kernel,
        grid=(depth, n // tn),
        in_specs=[
            pl.BlockSpec((SUBLANES, d), lambda l, j: (0, 0)),
            pl.BlockSpec((1, d, tn), lambda l, j: (l, 0, j)),
            pl.BlockSpec((1, 1, tn), lambda l, j: (l, 0, j)),
        ],
        out_specs=pl.BlockSpec((1, SUBLANES, tn), lambda l, j: (l, 0, j)),
        out_shape=jax.ShapeDtypeStruct((depth, SUBLANES, n), f32),
        compiler_params=_cparams(("arbitrary", "arbitrary")),
        name="modulation",
    )(cvec, mod_w, mod_b.reshape(depth, 1, n))


def _inproj_kernel(has_res, *refs):
    if has_res:
        h_ref, peer_ref, modp_ref, mod_ref, g_ref, w_ref, hout_ref, p_ref = refs
        h = h_ref[0] + modp_ref[0, 5:6, :] * peer_ref[0]
        hout_ref[0] = h
    else:
        h_ref, mod_ref, g_ref, w_ref, p_ref = refs
        h = h_ref[0]
    u = _rms(h, g_ref[...]) * (1.0 + mod_ref[0, 1:2, :]) + mod_ref[0, 0:1, :]
    p_ref[0] = jnp.dot(u.astype(bf16), w_ref[...], preferred_element_type=f32)


def inproj(h, mod, g, w, nctx_blk, peer=None, mod_prev=None):
    b, tt, d = h.shape
    n = w.shape[1]
    tr = ROW_BLOCK
    has_res = peer is not None
    row = pl.BlockSpec((1, tr, d), lambda bi, i: (bi, i, 0))
    modspec = pl.BlockSpec((1, SUBLANES, d), lambda bi, i: (_mod_row(bi, i, nctx_blk), 0, 0))
    in_specs = [row] + ([row, modspec] if has_res else []) + [
        modspec,
        pl.BlockSpec((1, d), lambda bi, i: (0, 0)),
        pl.BlockSpec((d, n), lambda bi, i: (0, 0)),
    ]
    pspec = pl.BlockSpec((1, tr, n), lambda bi, i: (bi, i, 0))
    pshape = jax.ShapeDtypeStruct((b, tt, n), f32)
    args = [h] + ([peer, mod_prev] if has_res else []) + [mod, g.reshape(1, d), w]
    return pl.pallas_call(
        functools.partial(_inproj_kernel, has_res),
        grid=(b, tt // tr),
        in_specs=in_specs,
        out_specs=[row, pspec] if has_res else pspec,
        out_shape=[jax.ShapeDtypeStruct((b, tt, d), f32), pshape] if has_res else pshape,
        compiler_params=_cparams(("arbitrary", "arbitrary")),
        name="inproj",
    )(*args)


def _conv_kernel(apply_silu, nctx_blk, nblk, prev_ref, cur_ref, next_ref, w_ref, b_ref, o_ref, ext):
    i = pl.program_id(1)
    tr = cur_ref.shape[1]
    has_prev = jnp.logical_and(i != 0, i != nctx_blk)
    has_next = jnp.logical_and(i != nctx_blk - 1, i != nblk - 1)
    ext[0:SUBLANES] = jnp.where(has_prev, prev_ref[0], 0.0)
    ext[SUBLANES:SUBLANES + tr] = cur_ref[0]
    ext[SUBLANES + tr:2 * SUBLANES + tr] = jnp.where(has_next, next_ref[0], 0.0)
    lo = CONV_W // 2
    y = b_ref[...] + w_ref[0:1, :] * ext[pl.ds(SUBLANES - lo, tr), :]
    for j in range(1, CONV_W):
        y = y + w_ref[j:j + 1, :] * ext[pl.ds(SUBLANES - lo + j, tr), :]
    if apply_silu:
        y = y * _sigmoid(y)
    o_ref[0] = y


def dwconv(p, c, w, bias, nctx_blk, apply_silu):
    b, tt, _ = p.shape
    tr = ROW_BLOCK
    nblk = tt // tr
    hb = tr // SUBLANES
    return pl.pallas_call(
        functools.partial(_conv_kernel, apply_silu, nctx_blk, nblk),
        grid=(b, nblk),
        in_specs=[
            pl.BlockSpec((1, SUBLANES, c), lambda bi, i: (bi, jnp.maximum(i * hb - 1, 0), 0)),
            pl.BlockSpec((1, tr, c), lambda bi, i: (bi, i, 0)),
            pl.BlockSpec((1, SUBLANES, c), lambda bi, i: (bi, jnp.minimum((i + 1) * hb, nblk * hb - 1), 0)),
            pl.BlockSpec((CONV_W, c), lambda bi, i: (0, 0)),
            pl.BlockSpec((1, c), lambda bi, i: (0, 0)),
        ],
        out_specs=pl.BlockSpec((1, tr, c), lambda bi, i: (bi, i, 0)),
        out_shape=jax.ShapeDtypeStruct((b, tt, c), f32),
        scratch_shapes=[pltpu.VMEM((tr + 2 * SUBLANES, c), f32)],
        compiler_params=_cparams(("arbitrary", "arbitrary")),
        name="dwconv",
    )(p, p, p, w, bias.reshape(1, c))


def _rglru_kernel(reverse, xc_ref, wa_ref, wx_ref, ba_ref, bx_ref, lam_ref, o_ref, carry):
    i = pl.program_id(1)

    @pl.when(i == 0)
    def _():
        carry[...] = jnp.zeros_like(carry)

    xc = xc_ref[0]
    tr = xc.shape[0]
    rs, gs = [], []
    for h in range(RG_HEADS):
        xh = xc[:, h * RG_HEAD_DIM:(h + 1) * RG_HEAD_DIM].astype(bf16)
        rs.append(jnp.dot(xh, wa_ref[h].astype(bf16), preferred_element_type=f32))
        gs.append(jnp.dot(xh, wx_ref[h].astype(bf16), preferred_element_type=f32))
    r = _sigmoid(jnp.concatenate(rs, axis=1) + ba_ref[...])
    g = _sigmoid(jnp.concatenate(gs, axis=1) + bx_ref[...])
    log_a = -RG_C * r * _softplus(-lam_ref[...])
    a = jnp.exp(log_a)
    u = jnp.sqrt(-jnp.tanh(log_a) * (a * a + 1.0)) * (g * xc)
    rowid = lax.broadcasted_iota(i32, (tr, 1), 0)
    s = 1
    while s < tr:
        if reverse:
            keep = rowid < tr - s
            shift = tr - s
        else:
            keep = rowid >= s
            shift = s
        a_sh = jnp.where(keep, pltpu.roll(a, shift, 0), 1.0)
        u_sh = jnp.where(keep, pltpu.roll(u, shift, 0), 0.0)
        u = u + a * u_sh
        a = a * a_sh
        s *= 2
    hseq = a * carry[0:1, :] + u
    o_ref[0] = hseq
    last = 0 if reverse else tr - 1
    carry[0:1, :] = hseq[last:last + 1, :]


def rglru_scan(xc, wa, wx, ba, bx, lam, nctx_blk, reverse):
    b, tt, c = xc.shape
    tr = ROW_BLOCK
    nblk = tt // tr
    if reverse:
        def blk(bi, i):
            return (bi, jnp.where(i < nctx_blk, nctx_blk - 1 - i, nblk - 1 - (i - nctx_blk)), 0)
    else:
        def blk(bi, i):
            return (bi, i, 0)
    vec = pl.BlockSpec((1, c), lambda bi, i: (0, 0))
    wspec = pl.BlockSpec((RG_HEADS, RG_HEAD_DIM, RG_HEAD_DIM), lambda bi, i: (0, 0, 0))
    return pl.pallas_call(
        functools.partial(_rglru_kernel, reverse),
        grid=(b, nblk),
        in_specs=[pl.BlockSpec((1, tr, c), blk), wspec, wspec, vec, vec, vec],
        out_specs=pl.BlockSpec((1, tr, c), blk),
        out_shape=jax.ShapeDtypeStruct((b, tt, c), f32),
        scratch_shapes=[pltpu.VMEM((SUBLANES, c), f32)],
        compiler_params=_cparams(("arbitrary", "arbitrary")),
        name="rglru_rev" if reverse else "rglru_fwd",
    )(xc, wa, wx, ba.reshape(1, c), bx.reshape(1, c), lam.reshape(1, c))


def _dft_kernel(t_len, x_ref, o_ref, c2, s2, rot):
    mi = pl.program_id(0)
    ki = pl.program_id(1)
    bm, bk = c2.shape
    w0 = 2.0 * math.pi / t_len

    @pl.when(ki == 0)
    def _():
        k = mi * bm + lax.broadcasted_iota(i32, (bm, bk), 0)
        a = lax.broadcasted_iota(i32, (bm, bk), 1)
        ang = ((k * a) & (t_len - 1)).astype(f32) * w0
        c2[...] = jnp.cos(ang)
        s2[...] = jnp.sin(ang)
        kcol = mi * bm + lax.broadcasted_iota(i32, (bm, 1), 0)
        angd = ((kcol * bk) & (t_len - 1)).astype(f32) * w0
        rot[0] = jnp.ones((bm, 1), f32)
        rot[1] = jnp.zeros((bm, 1), f32)
        rot[2] = jnp.cos(angd)
        rot[3] = jnp.sin(angd)
        o_ref[...] = jnp.zeros_like(o_ref)

    c1, s1, cd, sd = rot[0], rot[1], rot[2], rot[3]
    rot[0] = c1 * cd - s1 * sd
    rot[1] = s1 * cd + c1 * sd
    cm = (c1 * c2[...] - s1 * s2[...]).astype(bf16)
    sm = (s1 * c2[...] + c1 * s2[...]).astype(bf16)
    w = x_ref.shape[2]
    for b in range(x_ref.shape[0]):
        xb = x_ref[b].astype(bf16)
        o_ref[b, :, 0:w] += jnp.dot(cm, xb, preferred_element_type=f32)
        o_ref[b, :, w:2 * w] += jnp.dot(sm, xb, preferred_element_type=f32)


def dft_positions(p, col_blk, width, row0, t_len):
    b = p.shape[0]
    assert t_len & (t_len - 1) == 0
    bk = min(ROW_BLOCK, t_len)
    bm = min(1024, t_len)
    r0 = row0 // bk
    return pl.pallas_call(
        functools.partial(_dft_kernel, t_len),
        grid=(t_len // bm, t_len // bk),
        in_specs=[pl.BlockSpec((b, bk, width), lambda mi, ki: (0, r0 + ki, col_blk))],
        out_specs=pl.BlockSpec((b, bm, 2 * width), lambda mi, ki: (0, mi, 0)),
        out_shape=jax.ShapeDtypeStruct((b, t_len, 2 * width), f32),
        scratch_shapes=[pltpu.VMEM((bm, bk), f32), pltpu.VMEM((bm, bk), f32), pltpu.VMEM((4, bm, 1), f32)],
        compiler_params=_cparams(("arbitrary", "arbitrary")),
        name="dft_positions",
    )(p)


def _residual_and_ffn_norm(h, y, mod_ref, gf_ref, hout_ref, f_ref):
    hn = h + mod_ref[0, 2:3, :] * y
    hout_ref[0] = hn
    f_ref[0] = _rms(hn, gf_ref[...]) * (1.0 + mod_ref[0, 4:5, :]) + mod_ref[0, 3:4, :]


def _merge_even_kernel(nctx_blk, sc_ctx, sc_lat, hf_ref, hb_ref, gate_ref, z_ref, h_ref, mod_ref, cc_ref, cs_ref,
                       fw_ref, fb_ref, wa_ref, wb_ref, gf_ref, hout_ref, f_ref):
    i = pl.program_id(1)
    ya = (hf_ref[0] + hb_ref[0]) * _gelu_tanh(gate_ref[0])
    z = z_ref[0]
    w = z.shape[1] // 2
    zz = (jnp.dot(z[:, :w], cc_ref[...], precision=HIGHEST, preferred_element_type=f32)
          - jnp.dot(z[:, w:], cs_ref[...], precision=HIGHEST, preferred_element_type=f32))
    zz = zz * jnp.where(i < nctx_blk, sc_ctx, sc_lat)
    yb = jnp.dot(zz, fw_ref[...], precision=HIGHEST, preferred_element_type=f32) + fb_ref[...]
    y = (jnp.dot(ya.astype(bf16), wa_ref[...], preferred_element_type=f32)
         + jnp.dot(yb.astype(bf16), wb_ref[...], preferred_element_type=f32))
    _residual_and_ffn_norm(h_ref[0], y, mod_ref, gf_ref, hout_ref, f_ref)


def merge_even(hf, hb, p, z, h, mod, fw_bd, fb, w_out, gf, nctx_blk, t_ctx, t_lat):
    b, tt, d = h.shape
    tr = ROW_BLOCK
    jj, kk = np.meshgrid(np.arange(FN_GROUP_DIM), np.arange(FN_GROUP_DIM), indexing="ij")
    ang = 2.0 * np.pi * jj * kk / FN_GROUP_DIM
    eye = np.eye(FN_GROUPS)
    cc = jnp.asarray(np.kron(eye, np.cos(ang)), f32)
    cs = jnp.asarray(np.kron(eye, np.sin(ang)), f32)
    sc_ctx = 1.0 / math.sqrt(t_ctx * FN_GROUP_DIM)
    sc_lat = 1.0 / math.sqrt(t_lat * FN_GROUP_DIM)

    def rows(wd, cb=0):
        return pl.BlockSpec((1, tr, wd), lambda bi, i: (bi, i, cb))

    def full(shape):
        return pl.BlockSpec(shape, lambda bi, i: (0,) * len(shape))

    return pl.pallas_call(
        functools.partial(_merge_even_kernel, nctx_blk, sc_ctx, sc_lat),
        grid=(b, tt // tr),
        in_specs=[
            rows(RG_WIDTH), rows(RG_WIDTH), rows(RG_WIDTH, 1), rows(2 * FN_WIDTH), rows(d),
            pl.BlockSpec((1, SUBLANES, d), lambda bi, i: (_mod_row(bi, i, nctx_blk), 0, 0)),
            full((FN_WIDTH, FN_WIDTH)), full((FN_WIDTH, FN_WIDTH)), full((FN_WIDTH, FN_WIDTH)), full((1, FN_WIDTH)),
            full((RG_WIDTH, d)), full((FN_WIDTH, d)), full((1, d)),
        ],
        out_specs=[rows(d), rows(d)],
        out_shape=[jax.ShapeDtypeStruct((b, tt, d), f32)] * 2,
        compiler_params=_cparams(("arbitrary", "arbitrary")),
        name="merge_even",
    )(hf, hb, p, z, h, mod, cc, cs, fw_bd, fb.reshape(1, FN_WIDTH),
      w_out[:RG_WIDTH].astype(bf16), w_out[RG_WIDTH:].astype(bf16), gf.reshape(1, d))


def _merge_odd_kernel(hs0_ref, hs1_ref, o_ref, at_ref, h_ref, mod_ref, hg_ref, wm_ref, wa_ref, gf_ref, hout_ref, f_ref):
    hs = hs0_ref[0, 0] + hs1_ref[0, 0]
    parts = []
    for hd in range(ML_HEADS):
        sl = slice(hd * ML_HEAD_DIM, (hd + 1) * ML_HEAD_DIM)
        parts.append(_rms(hs[:, sl], hg_ref[:, sl]))
    yml = _sigmoid(o_ref[0]) * jnp.concatenate(parts, axis=1)
    y = (jnp.dot(yml.astype(bf16), wm_ref[...], preferred_element_type=f32)
         + jnp.dot(at_ref[0].astype(bf16), wa_ref[...], preferred_element_type=f32))
    _residual_and_ffn_norm(h_ref[0], y, mod_ref, gf_ref, hout_ref, f_ref)


def merge_odd(hs, p, attn, h, mod, head_g, w_ml, w_at, gf, nctx_blk):
    b, t, ap = attn.shape
    d = h.shape[2]
    tr = ROW_BLOCK

    def rows(wd, cb=0, off=0):
        return pl.BlockSpec((1, tr, wd), lambda bi, i: (bi, i + off, cb))

    def full(shape):
        return pl.BlockSpec(shape, lambda bi, i: (0,) * len(shape))

    return pl.pallas_call(
        _merge_odd_kernel,
        grid=(b, t // tr),
        in_specs=[
            pl.BlockSpec((1, 1, tr, ML_WIDTH), lambda bi, i: (0, bi, i + nctx_blk, 0)),
            pl.BlockSpec((1, 1, tr, ML_WIDTH), lambda bi, i: (1, bi, i + nctx_blk, 0)),
            rows(ML_WIDTH, OD_O // ML_WIDTH, nctx_blk), rows(ap), rows(d, 0, nctx_blk),
            pl.BlockSpec((1, SUBLANES, d), lambda bi, i: (bi, 0, 0)),
            full((1, ML_WIDTH)), full((ML_WIDTH, d)), full((ap, d)), full((1, d)),
        ],
        out_specs=[rows(d), rows(d)],
        out_shape=[jax.ShapeDtypeStruct((b, t, d), f32)] * 2,
        compiler_params=_cparams(("arbitrary", "arbitrary")),
        name="merge_odd",
    )(hs, hs, p, attn, h, mod, head_g.reshape(1, ML_WIDTH), w_ml, w_at, gf.reshape(1, d))


def _mlstm_kernel(qk_ref, v_ref, g_ref, gt_ref, gb_ref, gbt_ref, o_ref, ct, nst, mst):
    d = pl.program_id(0)
    c = pl.program_id(2)
    L = ML_CHUNK

    @pl.when(c == 0)
    def _():
        ct[...] = jnp.zeros_like(ct)
        nst[...] = jnp.zeros_like(nst)
        mst[...] = jnp.zeros_like(mst)

    ri = lax.broadcasted_iota(i32, (L, L), 0)
    ci = lax.broadcasted_iota(i32, (L, L), 1)
    sgn = 1 - 2 * d
    tri = (ci - ri) * sgn <= 0
    trit = (ri - ci) * sgn <= 0
    gcol = g_ref[0] + gb_ref[...]
    grow = gt_ref[0] + gbt_ref[...]
    lane = lax.broadcasted_iota(i32, (1, LANES), 1)
    for hd in range(ML_HEADS):
        sl = slice(hd * ML_HEAD_DIM, (hd + 1) * ML_HEAD_DIM)
        q = qk_ref[0, :, sl] * (ML_HEAD_DIM ** -0.5)
        k = qk_ref[0, :, ML_WIDTH + hd * ML_HEAD_DIM:ML_WIDTH + (hd + 1) * ML_HEAD_DIM]
        v = v_ref[0, :, sl]
        li = d * 8 + hd
        ig_col = jnp.sum(jnp.where(lane == li, gcol, 0.0), axis=1, keepdims=True)
        fg_col = jnp.sum(jnp.where(lane == li + 4, gcol, 0.0), axis=1, keepdims=True)
        rsel = lax.broadcasted_iota(i32, (ML_N_GATES, 1), 0)
        ig_row = jnp.sum(jnp.where(rsel == li, grow, 0.0), axis=0, keepdims=True)
        fg_row = jnp.sum(jnp.where(rsel == li + 4, grow, 0.0), axis=0, keepdims=True)
        lf_col = -_softplus(-fg_col)
        lf_row = -_softplus(-fg_row)
        b_col = jnp.sum(jnp.where(tri, lf_row, 0.0), axis=1, keepdims=True)
        b_row = jnp.sum(jnp.where(trit, lf_col, 0.0), axis=0, keepdims=True)
        m_prev = mst[hd, 0:1, 0:1]
        n_prev = nst[hd, 0:1, :]
        ct_prev = ct[hd]
        dm = jnp.where(tri, b_col - b_row + ig_row, -jnp.inf)
        inter = b_col + m_prev
        m_row = jnp.maximum(jnp.max(dm, axis=1, keepdims=True), inter)
        qb, kb, vb = q.astype(bf16), k.astype(bf16), v.astype(bf16)
        s = lax.dot_general(qb, kb, (((1,), (1,)), ((), ())), preferred_element_type=f32) * jnp.exp(dm - m_row)
        w_inter = jnp.exp(inter - m_row)
        num = (jnp.dot(s.astype(bf16), vb, preferred_element_type=f32)
               + w_inter * jnp.dot(qb, ct_prev.astype(bf16), preferred_element_type=f32))
        den = jnp.sum(s, axis=1, keepdims=True) + w_inter * jnp.sum(q * n_prev, axis=1, keepdims=True)
        o_ref[0, 0, :, sl] = num / jnp.maximum(jnp.abs(den), jnp.exp(-m_row))
        b_l = jnp.sum(lf_col, axis=0, keepdims=True)
        log_wk = b_l - b_col + ig_col
        m_new = jnp.maximum(b_l + m_prev, jnp.max(log_wk, axis=0, keepdims=True))
        kw = jnp.exp(log_wk - m_new) * k
        decay = jnp.exp(b_l + m_prev - m_new)
        ct[hd] = decay * ct_prev + lax.dot_general(kw.astype(bf16), vb, (((0,), (0,)), ((), ())),
                                                   preferred_element_type=f32)
        nst[hd, 0:1, :] = decay * n_prev + jnp.sum(kw, axis=0, keepdims=True)
        mst[hd] = jnp.broadcast_to(m_new, (SUBLANES, LANES))


def mlstm(qkc, p, gt, gate_b, nctx_chunks):
    b, tt, _ = qkc.shape
    L = ML_CHUNK
    nch = tt // L

    def chunk(d, c):
        return jnp.where(d == 0, c, jnp.where(c < nctx_chunks, nctx_chunks - 1 - c, nch - 1 - (c - nctx_chunks)))

    gb = jnp.zeros((1, LANES), f32).at[0, :ML_N_GATES].set(gate_b)
    return pl.pallas_call(
        _mlstm_kernel,
        grid=(2, b, nch),
        in_specs=[
            pl.BlockSpec((1, L, 2 * ML_WIDTH), lambda d, bi, c: (bi, chunk(d, c), 0)),
            pl.BlockSpec((1, L, ML_WIDTH), lambda d, bi, c: (bi, chunk(d, c), OD_V // ML_WIDTH)),
            pl.BlockSpec((1, L, LANES), lambda d, bi, c: (bi, chunk(d, c), OD_G // LANES)),
            pl.BlockSpec((1, ML_N_GATES, L), lambda d, bi, c: (bi, 0, chunk(d, c))),
            pl.BlockSpec((1, LANES), lambda d, bi, c: (0, 0)),
            pl.BlockSpec((ML_N_GATES, 1), lambda d, bi, c: (0, 0)),
        ],
        out_specs=pl.BlockSpec((1, 1, L, ML_WIDTH), lambda d, bi, c: (d, bi, chunk(d, c), 0)),
        out_shape=jax.ShapeDtypeStruct((2, b, tt, ML_WIDTH), f32),
        scratch_shapes=[
            pltpu.VMEM((ML_HEADS, ML_HEAD_DIM, ML_HEAD_DIM), f32),
            pltpu.VMEM((ML_HEADS, SUBLANES, LANES), f32),
            pltpu.VMEM((ML_HEADS, SUBLANES, LANES), f32),
        ],
        compiler_params=_cparams(("arbitrary", "arbitrary", "arbitrary")),
        name="mlstm",
    )(qkc, p, p, gt, gb, gate_b.reshape(ML_N_GATES, 1))


def _mla_prep_kernel(nctx_blk, p_ref, qg_ref, kg_ref, wq_ref, wqs_ref, wk_ref, wv_ref, em_ref, es_ref,
                     q_ref, k_ref, v_ref):
    i = pl.program_id(1)
    tr = p_ref.shape[1]
    cq = _rms(p_ref[0, :, OD_CQ:OD_CKV], qg_ref[...]).astype(bf16)
    ckv = _rms(p_ref[0, :, OD_CKV:OD_KR], kg_ref[...]).astype(bf16)
    kr = p_ref[0, :, OD_KR:OD_N]
    lane = lax.broadcasted_iota(i32, (1, MLA_HEAD_PAD), 1)
    half = MLA_ROPE // 2
    pair = jnp.where(lane < MLA_NOPE + half, lane - MLA_NOPE, lane - MLA_NOPE - half)
    is_rope = jnp.logical_and(lane >= MLA_NOPE, lane < MLA_NOPE + MLA_ROPE)
    use_row = pair < half // 2
    fidx = jnp.clip(jnp.where(use_row, pair, pair - half // 2), 0, half // 2 - 1).astype(f32)
    inv = jnp.exp(fidx * (-math.log(ROPE_THETA) / (half // 2)))
    t = (i - nctx_blk) * tr + lax.broadcasted_iota(i32, (tr, 1), 0)
    pos = jnp.where(use_row, (t >> GRID_SHIFT).astype(f32), (t & (GRID_W - 1)).astype(f32))
    ang = pos * inv
    rot = jnp.logical_and(is_rope, i >= nctx_blk)
    cos1 = jnp.where(rot, jnp.cos(ang), 1.0)
    sin1 = jnp.where(rot, jnp.sin(ang), 0.0)
    cos_t = jnp.concatenate([cos1] * MLA_HEADS, axis=1)
    sin_t = jnp.concatenate([sin1] * MLA_HEADS, axis=1)
    scale = (MLA_NOPE + MLA_ROPE) ** -0.5
    q = (jnp.dot(cq, wq_ref[...], preferred_element_type=f32) * cos_t
         + jnp.dot(cq, wqs_ref[...], preferred_element_type=f32) * sin_t)
    q_ref[0] = (q * scale).astype(bf16)
    k = (jnp.dot(ckv, wk_ref[...], preferred_element_type=f32)
         + jnp.dot(kr, em_ref[...], precision=HIGHEST, preferred_element_type=f32) * cos_t
         + jnp.dot(kr, es_ref[...], precision=HIGHEST, preferred_element_type=f32) * sin_t)
    k_ref[0] = k.astype(bf16)
    v_ref[0] = jnp.dot(ckv, wv_ref[...], preferred_element_type=f32).astype(bf16)


def _mla_weights(w_uq, w_ukv):
    hp = MLA_HEAD_PAD
    half = MLA_ROPE // 2
    wq = w_uq.reshape(MLA_Q_LORA, MLA_HEADS, MLA_NOPE + MLA_ROPE)
    nope, x1, x2 = wq[..., :MLA_NOPE], wq[..., MLA_NOPE::2], wq[..., MLA_NOPE + 1::2]
    zq = jnp.zeros((MLA_Q_LORA, MLA_HEADS, hp - MLA_NOPE - MLA_ROPE), f32)
    wq_main = jnp.concatenate([nope, x1, x2, zq], axis=-1).reshape(MLA_Q_LORA, MLA_HEADS * hp)
    wq_swap = jnp.concatenate([jnp.zeros_like(nope), -x2, x1, zq], axis=-1).reshape(MLA_Q_LORA, MLA_HEADS * hp)
    wkv = w_ukv.reshape(MLA_KV_LORA, MLA_HEADS, MLA_NOPE + MLA_V)
    zk = jnp.zeros((MLA_KV_LORA, MLA_HEADS, hp - MLA_NOPE), f32)
    wk = jnp.concatenate([wkv[..., :MLA_NOPE], zk], axis=-1).reshape(MLA_KV_LORA, MLA_HEADS * hp)
    wv = jnp.concatenate([wkv[..., MLA_NOPE:], jnp.zeros((MLA_KV_LORA, MLA_HEADS, hp - MLA_V), f32)],
                         axis=-1).reshape(MLA_KV_LORA, MLA_HEADS * hp)
    em = np.zeros((LANES, MLA_HEADS, hp), np.float32)
    es = np.zeros((LANES, MLA_HEADS, hp), np.float32)
    for j in range(half):
        em[2 * j, :, MLA_NOPE + j] = 1.0
        em[2 * j + 1, :, MLA_NOPE + half + j] = 1.0
        es[2 * j + 1, :, MLA_NOPE + j] = -1.0
        es[2 * j, :, MLA_NOPE + half + j] = 1.0
    em = jnp.asarray(em.reshape(LANES, MLA_HEADS * hp))
    es = jnp.asarray(es.reshape(LANES, MLA_HEADS * hp))
    return wq_main.astype(bf16), wq_swap.astype(bf16), wk.astype(bf16), wv.astype(bf16), em, es


def mla_prep(p, q_norm_g, kv_norm_g, w_uq, w_ukv, nctx_blk):
    b, tt, n = p.shape
    tr = ROW_BLOCK
    hw = MLA_HEADS * MLA_HEAD_PAD
    wq, wqs, wk, wv, em, es = _mla_weights(w_uq, w_ukv)

    def full(shape):
        return pl.BlockSpec(shape, lambda bi, i: (0,) * len(shape))

    out = pl.BlockSpec((1, tr, hw), lambda bi, i: (bi, i, 0))
    return pl.pallas_call(
        functools.partial(_mla_prep_kernel, nctx_blk),
        grid=(b, tt // tr),
        in_specs=[pl.BlockSpec((1, tr, n), lambda bi, i: (bi, i, 0)),
                  full((1, MLA_Q_LORA)), full((1, MLA_KV_LORA)),
                  full((MLA_Q_LORA, hw)), full((MLA_Q_LORA, hw)), full((MLA_KV_LORA, hw)), full((MLA_KV_LORA, hw)),
                  full((LANES, hw)), full((LANES, hw))],
        out_specs=[out, out, out],
        out_shape=[jax.ShapeDtypeStruct((b, tt, hw), bf16)] * 3,
        compiler_params=_cparams(("arbitrary", "arbitrary")),
        name="mla_prep",
    )(p, q_norm_g.reshape(1, MLA_Q_LORA), kv_norm_g.reshape(1, MLA_KV_LORA), wq, wqs, wk, wv, em, es)


def _mla_attn_kernel(q_ref, k_ref, v_ref, o_ref):
    tq = q_ref.shape[1]
    cq = tq // ATTN_Q_CHUNKS
    for c in range(ATTN_Q_CHUNKS):
        rows = slice(c * cq, (c + 1) * cq)
        s = lax.dot_general(q_ref[0, rows, :], k_ref[0], (((1,), (1,)), ((), ())), preferred_element_type=f32)
        m = jnp.max(s, axis=-1, keepdims=True)
        e = jnp.exp(s - m)
        l = jnp.sum(e, axis=-1, keepdims=True)
        o_ref[0, rows, :] = jnp.dot(e.astype(bf16), v_ref[0], preferred_element_type=f32) / l


def mla_attention(q, k, v, nctx_blk):
    b, tt, hw = q.shape
    tq = ROW_BLOCK
    nq = tt // tq - nctx_blk
    hp = MLA_HEAD_PAD
    kv = pl.BlockSpec((1, tt, hp), lambda bi, h, qi: (bi, 0, h))
    return pl.pallas_call(
        _mla_attn_kernel,
        grid=(b, MLA_HEADS, nq),
        in_specs=[pl.BlockSpec((1, tq, hp), lambda bi, h, qi: (bi, qi + nctx_blk, h)), kv, kv],
        out_specs=pl.BlockSpec((1, tq, hp), lambda bi, h, qi: (bi, qi, h)),
        out_shape=jax.ShapeDtypeStruct((b, nq * tq, hw), f32),
        compiler_params=_cparams(("arbitrary", "arbitrary", "arbitrary")),
        name="mla_attention",
    )(q, k, v)


def _topk_rows(s, k, payload=None):
    n = s.shape[0]
    rid = lax.broadcasted_iota(i32, s.shape, 0)
    vals, outs = [], []
    for _ in range(k):
        m = jnp.max(s, axis=0, keepdims=True)
        am = jnp.min(jnp.where(s == m, rid, n), axis=0, keepdims=True)
        hit = rid == am
        vals.append(m)
        outs.append(am if payload is None else jnp.sum(jnp.where(hit, payload, 0), axis=0, keepdims=True))
        s = jnp.where(hit, -jnp.inf, s)
    return jnp.concatenate(vals, axis=0), jnp.concatenate(outs, axis=0)


def _pruned_candidates(vals, idxs):
    k = PEER_TOPK
    hk = k // 2
    bcol = lax.broadcasted_iota(i32, (hk, 1), 0)
    cand = [vals[0][0:1, :] + vals[1]]
    cidx = [idxs[0][0:1, :] * PEER_N_KEYS + idxs[1]]
    for a in range(1, hk):
        ok = (a + 1) * (bcol + 1) <= k
        cand.append(jnp.where(ok, vals[0][a:a + 1, :] + vals[1][0:hk, :], -jnp.inf))
        cidx.append(idxs[0][a:a + 1, :] * PEER_N_KEYS + idxs[1][0:hk, :])
    cand.append(vals[0][hk:k, :] + vals[1][0:1, :])
    cidx.append(idxs[0][hk:k, :] * PEER_N_KEYS + idxs[1][0:1, :])
    return jnp.concatenate(cand, axis=0), jnp.concatenate(cidx, axis=0)


def _peer_route_kernel(f_ref, wq_ref, keys_ref, code_ref, g_ref):
    q = jnp.dot(f_ref[...].astype(bf16), wq_ref[...], preferred_element_type=f32)
    hk = PEER_KEY_DIM // 2
    for h in range(PEER_HEADS):
        vals, idxs = [], []
        for p in range(2):
            c0 = (h * 2 + p) * hk
            qhp = q[:, c0:c0 + hk].astype(bf16)
            st = lax.dot_general(keys_ref[p, h], qhp, (((1,), (1,)), ((), ())), preferred_element_type=f32)
            v, ix = _topk_rows(st, PEER_TOPK)
            vals.append(v)
            idxs.append(ix)
        cand, cidx = _pruned_candidates(vals, idxs)
        top_s, e = _topk_rows(cand, PEER_TOPK, payload=cidx)
        ex = jnp.exp(top_s - top_s[0:1, :])
        rows = slice(h * PEER_TOPK, (h + 1) * PEER_TOPK)
        g_ref[rows, :] = ex / jnp.sum(ex, axis=0, keepdims=True)
        code_ref[rows, :] = (e >> 1) * SUBLANES + ((1 - (e & 1)) << LOW_HALF_BIT)


def peer_route(f2, wq, keys):
    r, d = f2.shape
    tb = PEER_TB
    out = pl.BlockSpec((PEER_PICKS, tb), lambda i: (0, i))
    return pl.pallas_call(
        _peer_route_kernel,
        grid=(r // tb,),
        in_specs=[
            pl.BlockSpec((tb, d), lambda i: (i, 0)),
            pl.BlockSpec(wq.shape, lambda i: (0, 0)),
            pl.BlockSpec(keys.shape, lambda i: (0, 0, 0, 0)),
        ],
        out_specs=[out, out],
        out_shape=[jax.ShapeDtypeStruct((PEER_PICKS, r), i32), jax.ShapeDtypeStruct((PEER_PICKS, r), f32)],
        compiler_params=_cparams(("arbitrary",)),
        name="peer_route",
    )(f2, wq, keys)


def pack_expert_table(t):
    e, d = t.shape
    bits = lax.bitcast_convert_type(t.astype(bf16), jnp.uint16).astype(jnp.uint32)
    bits = bits.reshape(e // 2, 2, SUBLANES, d // SUBLANES)
    word = bits[:, 0] | (bits[:, 1] << 16)
    return lax.bitcast_convert_type(word, i32).reshape(e // 2 * SUBLANES, d // SUBLANES)


def _load_table_once(tab_hbm, tab_vmem, sem):
    @pl.when(pl.program_id(0) == 0)
    def _():
        cp = pltpu.make_async_copy(tab_hbm, tab_vmem, sem)
        cp.start()
        cp.wait()


def _slot_major(a, tb):
    r = a.shape[0]
    ng = PEER_PICKS // SUBLANES
    return a.reshape(r // tb, tb, ng, SUBLANES).transpose(0, 3, 1, 2).reshape(r // tb, SUBLANES, tb * ng)


def _fetch_block(src_hbm, dst_smems, sems):
    i = pl.program_id(0)
    n = dst_smems[0].shape[0] // 2
    half = i % 2

    def copies(blk, h):
        off = pl.multiple_of(h * n, n)
        return [pltpu.make_async_copy(src_hbm.at[blk, s], dst_smems[s].at[pl.ds(off, n)], sems.at[h, s])
                for s in range(SUBLANES)]

    @pl.when(i == 0)
    def _():
        for cp in copies(0, 0):
            cp.start()

    for cp in copies(i, half):
        cp.wait()

    @pl.when(i + 1 < pl.num_programs(0))
    def _():
        for cp in copies(i + 1, 1 - half):
            cp.start()

    return half * n


def _rotate_tiles(tb, mbufs, fill, consume):
    nb = len(mbufs)

    def rotate(p, c):
        for q in range(nb):
            consume(p * nb + q, mbufs[q])
            fill((p + 1) * nb + q, mbufs[q])
        return c

    for q in range(nb):
        fill(q, mbufs[q])
    lax.fori_loop(0, tb // nb - 1, rotate, 0)
    for q in range(nb):
        consume(tb - nb + q, mbufs[q])


def _peer_act_kernel(code_hbm, x_ref, g_ref, tab_hbm, ex_ref, rd_ref, wx_ref, tab_vmem, abuf, sem_tab, sem_a, *bufs):
    tb = x_ref.shape[0]
    ng = PEER_PICKS // SUBLANES
    nr = 2 * SUBLANES
    nx = PEER_PICKS // 2 * nr
    nb = PEER_TILE_DEPTH
    mbufs, code_s = bufs[:nb], bufs[nb:]
    _load_table_once(tab_hbm, tab_vmem, sem_tab)
    base = _fetch_block(code_hbm, code_s, sem_a)
    lane = lax.broadcasted_iota(i32, (nr, nx), 1)
    sub = lax.broadcasted_iota(i32, (nr, nx), 0)
    own = (lane & (nr - 1)) == sub
    half = SUBLANES // 2

    def pack_pairs(t, mbuf):
        for jg in range(ng):
            g = base + t * ng + jg
            for s in range(half):
                ca, cb = code_s[s][g], code_s[s + half][g]
                wa = tab_vmem[pl.ds(pl.multiple_of(ca & ROW_MASK, SUBLANES), SUBLANES), :]
                wb = tab_vmem[pl.ds(pl.multiple_of(cb & ROW_MASK, SUBLANES), SUBLANES), :]
                sha = 16 - (jnp.full((SUBLANES, LANES), ca, i32) >> 16)
                shb = jnp.full((SUBLANES, LANES), cb, i32) >> 16
                lo = lax.shift_right_logical(wa, sha) & 0xFFFF
                mbuf[pl.ds((jg * half + s) * SUBLANES, SUBLANES), :] = lo | ((wb << shb) & HI_MASK)

    def dots(t, mbuf):
        gm = lax.dot_general(x_ref[t], pltpu.bitcast(mbuf[...], bf16), (((1,), (1,)), ((), ())),
                             preferred_element_type=f32)
        abuf[pl.ds(t, 1), :] = jnp.sum(jnp.where(own, gm, 0.0), axis=0, keepdims=True)

    _rotate_tiles(tb, mbufs, pack_pairs, dots)
    a = abuf[...]
    a_hi = a.astype(bf16)
    a_lo = (a - a_hi.astype(f32)).astype(bf16)
    act = jnp.dot(a_hi, rd_ref[...], preferred_element_type=f32) + jnp.dot(a_lo, rd_ref[...], preferred_element_type=f32)
    w = g_ref[0] * _gelu_tanh(act)
    low = g_ref[1] > 0.5
    wx_ref[...] = (jnp.dot(jnp.where(low, w, 0.0).astype(bf16), ex_ref[0], preferred_element_type=f32)
                   + jnp.dot(jnp.where(low, 0.0, w).astype(bf16), ex_ref[1], preferred_element_type=f32))


def _expansion_matrices():
    ex = np.zeros((2, PEER_PICKS, PEER_PICKS * 2 * SUBLANES), np.float32)
    for j in range(PEER_PICKS):
        for s in range(SUBLANES):
            ex[0, j, j * 2 * SUBLANES + 2 * s] = 1.0
            ex[1, j, j * 2 * SUBLANES + 2 * s + 1] = 1.0
    return ex


def _pair_reduction_matrix():
    half = SUBLANES // 2
    rd = np.zeros((PEER_PICKS // 2 * 2 * SUBLANES, PEER_PICKS), np.float32)
    for jg in range(PEER_PICKS // SUBLANES):
        for s in range(SUBLANES):
            p = jg * half + s % half
            for q in range(SUBLANES):
                rd[p * 2 * SUBLANES + 2 * q + (s >= half), jg * SUBLANES + s] = 1.0
    return rd


def peer_act(code_sm, x2, glow, tab):
    r = x2.shape[0]
    tb = PEER_TB
    ng = PEER_PICKS // SUBLANES
    nx = PEER_PICKS * 2 * SUBLANES
    npair = PEER_PICKS // 2
    anyspec = pl.BlockSpec(memory_space=pl.ANY)
    return pl.pallas_call(
        _peer_act_kernel,
        grid=(r // tb,),
        in_specs=[anyspec, pl.BlockSpec((tb, 2 * SUBLANES, LANES), lambda i: (i, 0, 0)),
                  pl.BlockSpec((2, tb, PEER_PICKS), lambda i: (0, i, 0)), anyspec,
                  pl.BlockSpec((2, PEER_PICKS, nx), lambda i: (0, 0, 0)),
                  pl.BlockSpec((npair * 2 * SUBLANES, PEER_PICKS), lambda i: (0, 0))],
        out_specs=pl.BlockSpec((tb, nx), lambda i: (i, 0)),
        out_shape=jax.ShapeDtypeStruct((r, nx), f32),
        scratch_shapes=[
            pltpu.VMEM(tab.shape, i32),
            pltpu.VMEM((tb, npair * 2 * SUBLANES), f32),
            pltpu.SemaphoreType.DMA, pltpu.SemaphoreType.DMA((2, SUBLANES)),
        ] + [pltpu.VMEM((npair * SUBLANES, LANES), i32)] * PEER_TILE_DEPTH
          + [pltpu.SMEM((2 * tb * ng,), i32)] * SUBLANES,
        compiler_params=_cparams(("arbitrary",), VMEM_LIMIT_TABLE),
        name="peer_act",
    )(code_sm, x2, glow, tab, jnp.asarray(_expansion_matrices(), bf16), jnp.asarray(_pair_reduction_matrix(), bf16))


def _peer_out_kernel(row_hbm, wx_ref, tab_hbm, o_ref, tab_vmem, sem_tab, sem_a, *bufs):
    i = pl.program_id(0)
    tb = o_ref.shape[0]
    ng = PEER_PICKS // SUBLANES
    nx = wx_ref.shape[1]
    nb = PEER_TILE_DEPTH
    mbufs, row_s = bufs[:nb], bufs[nb:]
    _load_table_once(tab_hbm, tab_vmem, sem_tab)
    row_base = _fetch_block(row_hbm, row_s, sem_a)
    lane = lax.broadcasted_iota(i32, (SUBLANES, nx), 1)
    sub = lax.broadcasted_iota(i32, (SUBLANES, nx), 0)
    diag = ((lane & (2 * SUBLANES - 1)) >> 1) == sub

    def contract(t, mbuf):
        wsel = jnp.where(diag, jnp.broadcast_to(wx_ref[pl.ds(t, 1), :], (SUBLANES, nx)), 0.0).astype(bf16)
        o_ref[t] = jnp.dot(wsel, pltpu.bitcast(mbuf[...], bf16), preferred_element_type=f32)

    def copy_tiles(t, mbuf):
        for jg in range(ng):
            g = row_base + t * ng + jg
            for s in range(SUBLANES):
                row = pl.multiple_of(row_s[s][g], SUBLANES)
                mbuf[pl.ds((jg * SUBLANES + s) * SUBLANES, SUBLANES), :] = tab_vmem[pl.ds(row, SUBLANES), :]

    _rotate_tiles(tb, mbufs, copy_tiles, contract)


def peer_out(row_sm, wx, tab):
    nblk, _, n = row_sm.shape
    ng = PEER_PICKS // SUBLANES
    tb = n // ng
    nx = wx.shape[1]
    assert tb % PEER_TILE_DEPTH == 0
    anyspec = pl.BlockSpec(memory_space=pl.ANY)
    return pl.pallas_call(
        _peer_out_kernel,
        grid=(nblk,),
        in_specs=[anyspec, pl.BlockSpec((tb, nx), lambda i: (i, 0)), anyspec],
        out_specs=pl.BlockSpec((tb, SUBLANES, LANES), lambda i: (i, 0, 0)),
        out_shape=jax.ShapeDtypeStruct((nblk * tb, SUBLANES, LANES), f32),
        scratch_shapes=[
            pltpu.VMEM(tab.shape, i32),
            pltpu.SemaphoreType.DMA, pltpu.SemaphoreType.DMA((2, SUBLANES)),
        ] + [pltpu.VMEM((PEER_PICKS * SUBLANES, LANES), i32)] * PEER_TILE_DEPTH
          + [pltpu.SMEM((2 * n,), i32)] * SUBLANES,
        compiler_params=_cparams(("arbitrary",), VMEM_LIMIT_TABLE),
        name="peer_out",
    )(row_sm, wx, tab)


def peer_ffn(f, wq, keys, u, v):
    b, r, d = f.shape
    f2 = f.reshape(b * r, d)
    codet, gt = peer_route(f2, wq.astype(bf16), keys.astype(bf16))
    code = codet.T
    low = ((code >> LOW_HALF_BIT) & 1).astype(f32)
    x2 = jnp.repeat(f2.reshape(b * r, SUBLANES, d // SUBLANES), 2, axis=1).astype(bf16)
    wx = peer_act(_slot_major(code, PEER_TB), x2, jnp.stack([gt.T, low]), pack_expert_table(u))
    o = peer_out(_slot_major(code & ROW_MASK, PEER_TB), wx, pack_expert_table(v))
    return o.reshape(b, r, d)


def _final_kernel(h_ref, peer_ref, mod_ref, g_ref, o_ref):
    h = h_ref[0] + mod_ref[0, 5:6, :] * peer_ref[0]
    o_ref[0] = _rms(h, g_ref[...])


def final_norm(h, peer, mod, g):
    b, t, d = h.shape
    tr = ROW_BLOCK
    row = pl.BlockSpec((1, tr, d), lambda bi, i: (bi, i, 0))
    return pl.pallas_call(
        _final_kernel,
        grid=(b, t // tr),
        in_specs=[row, row, pl.BlockSpec((1, SUBLANES, d), lambda bi, i: (bi, 0, 0)),
                  pl.BlockSpec((1, d), lambda bi, i: (0, 0))],
        out_specs=row,
        out_shape=jax.ShapeDtypeStruct((b, t, d), f32),
        compiler_params=_cparams(("arbitrary", "arbitrary")),
        name="final_norm",
    )(h, peer, mod, g.reshape(1, d))


def _block_diag(w):
    g, a, c = w.shape
    out = jnp.zeros((g * a, g * c), w.dtype)
    for k in range(g):
        out = out.at[k * a:(k + 1) * a, k * c:(k + 1) * c].set(w[k])
    return out


def kernel(x, c, ctx, c_ctx, mod_w, mod_b, norm_mix_g, norm_ffn_g, ev_w_in, ev_w_out, rg_conv_w, rg_conv_b, rg_wa, rg_ba, rg_wx, rg_bx, rg_lam, fn_w, fn_b, od_w_in, od_w_out, ml_conv_w, ml_conv_b, ml_gate_b, ml_head_g, mla_q_norm_g, mla_w_uq, mla_kv_norm_g, mla_w_ukv, peer_w_q, peer_keys, peer_u, peer_v, final_g):
    b, t, d = x.shape
    tc = ctx.shape[1]
    assert b == 2 and tc % ROW_BLOCK == 0 and t % ROW_BLOCK == 0 and mod_w.shape[0] == 2
    nctx_blk = tc // ROW_BLOCK

    cvec = jnp.zeros((SUBLANES, d), f32).at[0:b].set(c).at[b].set(c_ctx)
    mods = modulation(cvec, mod_w, mod_b)
    mods = mods.reshape(2, SUBLANES, 6, d)[:, :3]
    mods = jnp.pad(mods, ((0, 0), (0, 0), (0, 2), (0, 0)))

    h0 = jnp.concatenate([ctx, x], axis=1)

    p = inproj(h0, mods[0], norm_mix_g[0], ev_w_in[0].astype(bf16), nctx_blk)
    xc = dwconv(p, RG_WIDTH, rg_conv_w[0], rg_conv_b[0], nctx_blk, apply_silu=False)
    hf = rglru_scan(xc, rg_wa[0, 0], rg_wx[0, 0], rg_ba[0, 0], rg_bx[0, 0], rg_lam[0, 0], nctx_blk, reverse=False)
    hb = rglru_scan(xc, rg_wa[0, 1], rg_wx[0, 1], rg_ba[0, 1], rg_bx[0, 1], rg_lam[0, 1], nctx_blk, reverse=True)
    fcol = 2 * RG_WIDTH // FN_WIDTH
    z = jnp.concatenate([dft_positions(p, fcol, FN_WIDTH, 0, tc), dft_positions(p, fcol, FN_WIDTH, tc, t)], axis=1)
    h1, f1 = merge_even(hf, hb, p, z, h0, mods[0], _block_diag(fn_w[0]), fn_b[0], ev_w_out[0], norm_ffn_g[0],
                        nctx_blk, tc, t)
    peer1 = peer_ffn(f1, peer_w_q[0], peer_keys[0], peer_u[0], peer_v[0])

    w = od_w_in[0]
    zpad = lambda n: jnp.zeros((d, n), f32)
    w_od = jnp.concatenate([w[:, :OD_G + ML_N_GATES], zpad(OD_CQ - OD_G - ML_N_GATES),
                            w[:, 2064:2064 + MLA_Q_LORA + MLA_KV_LORA + MLA_ROPE],
                            zpad(OD_N - OD_KR - MLA_ROPE)], axis=1).astype(bf16)
    h2, p2 = inproj(h1, mods[1], norm_mix_g[1], w_od, nctx_blk, peer=peer1, mod_prev=mods[0])
    qkc = dwconv(p2, 2 * ML_WIDTH, ml_conv_w[0], ml_conv_b[0], nctx_blk, apply_silu=True)
    gt = jnp.swapaxes(p2[:, :, OD_G:OD_G + ML_N_GATES], 1, 2)
    hs = mlstm(qkc, p2, gt, ml_gate_b[0], tc // ML_CHUNK)
    q, k, v = mla_prep(p2, mla_q_norm_g[0], mla_kv_norm_g[0], mla_w_uq[0], mla_w_ukv[0], nctx_blk)
    attn = mla_attention(q, k, v, nctx_blk)
    wo = od_w_out[0]
    w_at = jnp.concatenate([wo[ML_WIDTH:].reshape(MLA_HEADS, MLA_V, d),
                            jnp.zeros((MLA_HEADS, MLA_HEAD_PAD - MLA_V, d), f32)], axis=1)
    w_at = w_at.reshape(MLA_HEADS * MLA_HEAD_PAD, d).astype(bf16)
    h3, f3 = merge_odd(hs, p2, attn, h2, mods[1, :b], ml_head_g[0], wo[:ML_WIDTH].astype(bf16), w_at,
                       norm_ffn_g[1], nctx_blk)
    peer3 = peer_ffn(f3, peer_w_q[1], peer_keys[1], peer_u[1], peer_v[1])
    return final_norm(h3, peer3, mods[1, :b], final_g)
```
